```python
import jax, jax.numpy as jnp
from jax import lax
import numpy as np

D_MODEL = 4096
BATCH = 1
SEQ = 8192
DEPTH = 1

N_META = 16
GLA_HEADS = 8
GLA_DK = 128
GLA_DV = 256
GLA_GATE_RANK = 16
GLA_GATE_NORM = 16.0
GLA_CHUNK = 64
MLA_HEADS = 16
MLA_Q_RANK = 1024
MLA_KV_RANK = 512
MLA_NOPE = 128
MLA_ROPE = 64
MLA_V = 128
ROPE_THETA = 10000.0
Q_BLOCK = 128
D_FF = 4 * D_MODEL
EPS = 1e-6

GLA_QK_W = GLA_HEADS * GLA_DK
GLA_V_W = GLA_HEADS * GLA_DV
MLA_QK_HEAD = MLA_NOPE + MLA_ROPE
MLA_V_W = MLA_HEADS * MLA_V
IN_SPLITS = (GLA_QK_W, GLA_QK_W, GLA_V_W, GLA_V_W, GLA_GATE_RANK, GLA_GATE_RANK,
             MLA_Q_RANK, MLA_KV_RANK, MLA_ROPE, D_MODEL, D_MODEL)
IN_WIDTH = sum(IN_SPLITS)

kernel_name = "bidir_gla_mla_gated_hybrid"


def rmsnorm(x, g):
    xf = x.astype(jnp.float32)
    y = xf * lax.rsqrt(jnp.mean(xf * xf, axis=-1, keepdims=True) + EPS)
    return (y * g.astype(jnp.float32)).astype(x.dtype)


def apply_rope(x, cos, sin):
    xf = x.astype(jnp.float32)
    x1, x2 = jnp.split(xf, 2, axis=-1)
    c = cos[:, None, :]
    s = sin[:, None, :]
    return jnp.concatenate([x1 * c - x2 * s, x1 * s + x2 * c], axis=-1).astype(x.dtype)


def gla_direction(q, k, v, logg):
    B, H, T, dk = q.shape
    dv = v.shape[-1]
    C = GLA_CHUNK
    n = T // C

    def chunks(a):
        return jnp.moveaxis(a.reshape(B, H, n, C, a.shape[-1]), 2, 0)

    qc, kc, vc = chunks(q), chunks(k), chunks(v)
    bc = jnp.cumsum(chunks(logg), axis=-2)
    mask = jnp.tril(jnp.ones((C, C), dtype=bool))[:, :, None]

    def step(S, inp):
        qi, ki, vi, bi = inp
        inter = jnp.einsum('bhtd,bhde->bhte', qi * jnp.exp(bi), S)
        diff = bi[:, :, :, None, :] - bi[:, :, None, :, :]
        decay = jnp.exp(jnp.where(mask, diff, -jnp.inf))
        att = jnp.einsum('bhtsd,bhsd->bhts', qi[:, :, :, None, :] * decay, ki)
        intra = jnp.einsum('bhts,bhse->bhte', att, vi)
        last = bi[:, :, -1:, :]
        S_new = jnp.exp(last[:, :, 0, :])[..., None] * S + jnp.einsum(
            'bhsd,bhse->bhde', ki * jnp.exp(last - bi), vi)
        return S_new, inter + intra

    S0 = jnp.zeros((B, H, dk, dv), jnp.float32)
    _, out = lax.scan(step, S0, (qc, kc, vc, bc))
    return jnp.moveaxis(out, 0, 2).reshape(B, H, T, dv)


def block_attention(q, k, v):
    B, L, H, Dq = q.shape
    nb = -(-L // Q_BLOCK)
    Lq = nb * Q_BLOCK
    scale = Dq ** -0.5
    qb = jnp.pad(q, ((0, 0), (0, Lq - L), (0, 0), (0, 0)))
    qb = qb.reshape(B, nb, Q_BLOCK, H, Dq).transpose(1, 0, 3, 2, 4)
    kt = k.transpose(0, 2, 1, 3)
    vt = v.transpose(0, 2, 1, 3)

    def one(qi):
        s = jnp.einsum('bhqd,bhkd->bhqk', qi, kt, preferred_element_type=jnp.float32) * scale
        p = jax.nn.softmax(s, axis=-1)
        return jnp.einsum('bhqk,bhkd->bhqd', p.astype(vt.dtype), vt)

    o = lax.map(one, qb)
    return o.transpose(1, 0, 3, 2, 4).reshape(B, Lq, H, -1)[:, :L]


def hybrid_layer(h, cos, sin, ln1, w_in, gla_wf, gla_bf, gla_wb, gla_bb, gla_norm,
                 q_norm, w_uq, kv_norm, w_ukv, w_gla_out, w_mla_out, w_o,
                 ln2, w_ff1, w_ff2):
    B, L, _ = h.shape
    f32 = jnp.float32
    xn = rmsnorm(h, ln1)
    proj = xn @ w_in
    split_idx = [int(i) for i in np.cumsum(IN_SPLITS)[:-1]]
    (gq, gk, gv, gog, glrf, glrb, cq, ckv, krope, gate_a, gate_b) = jnp.split(proj, split_idx, axis=-1)

    pad = GLA_CHUNK - N_META

    def heads(a, d):
        a = a.reshape(B, L, -1, d).transpose(0, 2, 1, 3).astype(f32)
        return jnp.pad(a, ((0, 0), (0, 0), (pad, 0), (0, 0)))

    q = heads(gq, GLA_DK) * (GLA_DK ** -0.5)
    k = heads(gk, GLA_DK)
    v = heads(gv, GLA_DV)
    lg_f = heads(jax.nn.log_sigmoid((glrf @ gla_wf + gla_bf).astype(f32)) / GLA_GATE_NORM, GLA_DK)
    lg_b = heads(jax.nn.log_sigmoid((glrb @ gla_wb + gla_bb).astype(f32)) / GLA_GATE_NORM, GLA_DK)
    flip = lambda a: jnp.flip(a, axis=2)
    o_f = gla_direction(q, k, v, lg_f)
    o_b = flip(gla_direction(flip(q), flip(k), flip(v), flip(lg_b)))
    o = (o_f + o_b)[:, :, pad:, :].transpose(0, 2, 1, 3)
    o = rmsnorm(o, gla_norm).reshape(B, L, GLA_V_W).astype(h.dtype)
    y_gla = (o * jax.nn.silu(gog)) @ w_gla_out

    qf = (rmsnorm(cq, q_norm) @ w_uq).reshape(B, L, MLA_HEADS, MLA_QK_HEAD)
    q_nope, q_pe = qf[..., :MLA_NOPE], qf[..., MLA_NOPE:]
    q_pe = apply_rope(q_pe, cos, sin)
    kv = (rmsnorm(ckv, kv_norm) @ w_ukv).reshape(B, L, MLA_HEADS, MLA_NOPE + MLA_V)
    k_nope, v_m = kv[..., :MLA_NOPE], kv[..., MLA_NOPE:]
    k_pe = apply_rope(krope[:, :, None, :], cos, sin)
    qh = jnp.concatenate([q_nope, q_pe], axis=-1)
    kh = jnp.concatenate([k_nope, jnp.broadcast_to(k_pe, (B, L, MLA_HEADS, MLA_ROPE))], axis=-1)
    y_mla = block_attention(qh, kh, v_m).reshape(B, L, MLA_V_W) @ w_mla_out

    merged = jax.nn.sigmoid(gate_a) * y_gla + jax.nn.sigmoid(gate_b) * y_mla
    h = h + merged @ w_o

    hn = rmsnorm(h, ln2)
    h = h + jnp.square(jax.nn.relu(hn @ w_ff1)) @ w_ff2
    return h


def setup_inputs(seed: int = 0) -> dict:
    key = jax.random.key(seed)
    ks = jax.random.split(key, 24)
    f32 = jnp.float32

    def nrm(k, shape, scale):
        return jax.random.normal(k, shape, f32) * scale

    def gain(k, shape):
        return 1.0 + 0.01 * jax.random.normal(k, shape, f32)

    return {
        "x": nrm(ks[0], (BATCH, SEQ, D_MODEL), 1.0),
        "meta_tokens": nrm(ks[1], (N_META, D_MODEL), 1.0),
        "ln1": gain(ks[2], (DEPTH, D_MODEL)),
        "w_in": nrm(ks[3], (DEPTH, D_MODEL, IN_WIDTH), D_MODEL ** -0.5),
        "gla_wf": nrm(ks[4], (DEPTH, GLA_GATE_RANK, GLA_QK_W), GLA_GATE_RANK ** -0.5),
        "gla_bf": nrm(ks[5], (DEPTH, GLA_QK_W), 0.01),
        "gla_wb": nrm(ks[6], (DEPTH, GLA_GATE_RANK, GLA_QK_W), GLA_GATE_RANK ** -0.5),
        "gla_bb": nrm(ks[7], (DEPTH, GLA_QK_W), 0.01),
        "gla_norm": gain(ks[8], (DEPTH, GLA_DV)),
        "q_norm": gain(ks[9], (DEPTH, MLA_Q_RANK)),
        "w_uq": nrm(ks[10], (DEPTH, MLA_Q_RANK, MLA_HEADS * MLA_QK_HEAD), MLA_Q_RANK ** -0.5),
        "kv_norm": gain(ks[11], (DEPTH, MLA_KV_RANK)),
        "w_ukv": nrm(ks[12], (DEPTH, MLA_KV_RANK, MLA_HEADS * (MLA_NOPE + MLA_V)), MLA_KV_RANK ** -0.5),
        "w_gla_out": nrm(ks[13], (DEPTH, GLA_V_W, D_MODEL), GLA_V_W ** -0.5),
        "w_mla_out": nrm(ks[14], (DEPTH, MLA_V_W, D_MODEL), MLA_V_W ** -0.5),
        "w_o": nrm(ks[15], (DEPTH, D_MODEL, D_MODEL), D_MODEL ** -0.5),
        "ln2": gain(ks[16], (DEPTH, D_MODEL)),
        "w_ff1": nrm(ks[17], (DEPTH, D_MODEL, D_FF), D_MODEL ** -0.5),
        "w_ff2": nrm(ks[18], (DEPTH, D_FF, D_MODEL), D_FF ** -0.5),
        "final_norm": gain(ks[19], (D_MODEL,)),
    }


def reference(x, meta_tokens, ln1, w_in, gla_wf, gla_bf, gla_wb, gla_bb, gla_norm,
              q_norm, w_uq, kv_norm, w_ukv, w_gla_out, w_mla_out, w_o,
              ln2, w_ff1, w_ff2, final_norm):
    B = x.shape[0]
    meta = jnp.broadcast_to(meta_tokens[None].astype(x.dtype), (B, N_META, D_MODEL))
    h = jnp.concatenate([meta, x], axis=1)
    L = h.shape[1]
    pos = jnp.arange(L, dtype=jnp.float32)
    inv_freq = ROPE_THETA ** (-jnp.arange(0, MLA_ROPE, 2, dtype=jnp.float32) / MLA_ROPE)
    ang = pos[:, None] * inv_freq[None, :]
    cos, sin = jnp.cos(ang), jnp.sin(ang)
    for i in range(DEPTH):
        h = hybrid_layer(h, cos, sin, ln1[i], w_in[i], gla_wf[i], gla_bf[i], gla_wb[i], gla_bb[i],
                         gla_norm[i], q_norm[i], w_uq[i], kv_norm[i], w_ukv[i], w_gla_out[i],
                         w_mla_out[i], w_o[i], ln2[i], w_ff1[i], w_ff2[i])
    return rmsnorm(h, final_norm)[:, N_META:]
```

```python
import functools
import math

import jax
import jax.numpy as jnp
import numpy as np
from jax import lax
from jax.experimental import pallas as pl
from jax.experimental.pallas import tpu as pltpu

F32 = jnp.float32
BF16 = jnp.bfloat16

GLA_DK = 128
GLA_DV = 256
GLA_GATE_NORM = 16.0
MLA_NOPE = 128
MLA_ROPE = 64
MLA_V = 128
ROPE_THETA = 10000.0
EPS = 1e-6

LANES = 128
GLA_CHUNK = 128
GLA_LEVELS = (64, 32, 16, 8, 4, 2, 1)
NEG_BIG = -1e30
VMEM_LIMIT = 56 * 1024 * 1024


def _cparams(sem):
    return pltpu.CompilerParams(dimension_semantics=sem, vmem_limit_bytes=VMEM_LIMIT)


def _tile(n, pref):
    t = min(n, pref)
    while n % t:
        t //= 2
    return t


def _rms(a, g, eps=EPS):
    ms = jnp.mean(a * a, axis=-1, keepdims=True)
    return a * lax.rsqrt(ms + eps) * g


def _nmm_kernel(a_ref, g_ref, b_ref, o_ref, an_ref, *, act):
    @pl.when(pl.program_id(1) == 0)
    def _():
        an_ref[...] = _rms(a_ref[...], g_ref[...]).astype(BF16)

    y = jnp.dot(an_ref[...], b_ref[...], preferred_element_type=F32)
    if act == "relu2":
        y = jnp.square(jnp.maximum(y, 0.0))
    o_ref[...] = y.astype(o_ref.dtype)


def _norm_matmul(a, g, b, *, out_dtype, n_out=None, col_off=0, tm=512, tn=1024, act=None, name):
    M, K = a.shape
    n_out = b.shape[1] if n_out is None else n_out
    tm = _tile(M, tm)
    tn = _tile(n_out, tn)
    return pl.pallas_call(
        functools.partial(_nmm_kernel, act=act),
        grid=(M // tm, n_out // tn),
        in_specs=[
            pl.BlockSpec((tm, K), lambda i, j: (i, 0)),
            pl.BlockSpec((1, K), lambda i, j: (0, 0)),
            pl.BlockSpec((K, tn), lambda i, j: (0, j + col_off)),
        ],
        out_specs=pl.BlockSpec((tm, tn), lambda i, j: (i, j)),
        out_shape=jax.ShapeDtypeStruct((M, n_out), out_dtype),
        scratch_shapes=[pltpu.VMEM((tm, K), BF16)],
        compiler_params=_cparams(("arbitrary", "arbitrary")),
        name=name,
    )(a, g, b)


def _q_up_kernel(cq_ref, g_ref, w_ref, cos_ref, sin_ref, qt_ref, *, heads, scale):
    cn = _rms(cq_ref[...], g_ref[...]).astype(BF16)
    y = jnp.dot(cn, w_ref[...], preferred_element_type=F32) * scale
    c = cos_ref[...]
    s = sin_ref[...]
    half = MLA_ROPE // 2
    for h in range(heads):
        qt_ref[h, 0:MLA_NOPE, :] = y[:, h * MLA_NOPE:(h + 1) * MLA_NOPE].T.astype(BF16)
    rope0 = heads * MLA_NOPE
    for p in range(heads // 2):
        blk = y[:, rope0 + p * LANES: rope0 + (p + 1) * LANES].T
        for u in range(2):
            x1 = blk[u * MLA_ROPE: u * MLA_ROPE + half]
            x2 = blk[u * MLA_ROPE + half: (u + 1) * MLA_ROPE]
            h = 2 * p + u
            qt_ref[h, MLA_NOPE:MLA_NOPE + half, :] = (x1 * c - x2 * s).astype(BF16)
            qt_ref[h, MLA_NOPE + half:MLA_NOPE + MLA_ROPE, :] = (x1 * s + x2 * c).astype(BF16)
    pad = qt_ref.shape[1] - MLA_NOPE - MLA_ROPE
    for h in range(heads):
        qt_ref[h, MLA_NOPE + MLA_ROPE:, :] = jnp.zeros((pad, qt_ref.shape[2]), BF16)


def _q_up(proj_b, cq_blk, g, w, cos_t, sin_t, *, heads, tm=512):
    M = proj_b.shape[0]
    R = g.shape[1]
    tm = _tile(M, tm)
    kq = 2 * LANES
    return pl.pallas_call(
        functools.partial(_q_up_kernel, heads=heads, scale=float((MLA_NOPE + MLA_ROPE) ** -0.5)),
        grid=(M // tm,),
        in_specs=[
            pl.BlockSpec((tm, R), lambda i: (i, cq_blk)),
            pl.BlockSpec((1, R), lambda i: (0, 0)),
            pl.BlockSpec(w.shape, lambda i: (0, 0)),
            pl.BlockSpec((MLA_ROPE // 2, tm), lambda i: (0, i)),
            pl.BlockSpec((MLA_ROPE // 2, tm), lambda i: (0, i)),
        ],
        out_specs=pl.BlockSpec((heads, kq, tm), lambda i: (0, 0, i)),
        out_shape=jax.ShapeDtypeStruct((heads, kq, M), BF16),
        compiler_params=_cparams(("arbitrary",)),
        name="mla_q_up",
    )(proj_b, g, w, cos_t, sin_t)


def _kv_up_kernel(ckv_ref, kr_ref, g_ref, w_ref, cos_ref, sin_ref, k_ref, vt_ref, *, heads):
    cn = _rms(ckv_ref[...], g_ref[...]).astype(BF16)
    y = jnp.dot(cn, w_ref[...], preferred_element_type=F32)
    x1 = kr_ref[:, 0:LANES]
    x2 = kr_ref[:, LANES:2 * LANES]
    c = cos_ref[...]
    s = sin_ref[...]
    kpe = (x1 * c - x2 * s) + pltpu.roll(x1 * s + x2 * c, MLA_ROPE // 2, axis=1)
    kpe = kpe.astype(BF16)
    hw = MLA_NOPE + MLA_V
    for h in range(heads):
        k_ref[h, :, 0:MLA_NOPE] = y[:, h * hw: h * hw + MLA_NOPE].astype(BF16)
        k_ref[h, :, MLA_NOPE:] = kpe
        vt_ref[h, 0] = y[:, h * hw + MLA_NOPE:(h + 1) * hw].T.astype(BF16)


def _kv_up(proj_b, g, w, cos_p, sin_p, *, heads, ckv_blk, kr_blk, tm):
    M = proj_b.shape[0]
    R = g.shape[1]
    return pl.pallas_call(
        functools.partial(_kv_up_kernel, heads=heads),
        grid=(M // tm,),
        in_specs=[
            pl.BlockSpec((tm, R), lambda i: (i, ckv_blk)),
            pl.BlockSpec((tm, 2 * LANES), lambda i: (i, kr_blk)),
            pl.BlockSpec((1, R), lambda i: (0, 0)),
            pl.BlockSpec(w.shape, lambda i: (0, 0)),
            pl.BlockSpec((tm, LANES), lambda i: (i, 0)),
            pl.BlockSpec((tm, LANES), lambda i: (i, 0)),
        ],
        out_specs=[
            pl.BlockSpec((heads, tm, 2 * LANES), lambda i: (0, i, 0)),
            pl.BlockSpec((heads, 1, MLA_V, tm), lambda i: (0, i, 0, 0)),
        ],
        out_shape=[
            jax.ShapeDtypeStruct((heads, M, 2 * LANES), BF16),
            jax.ShapeDtypeStruct((heads, M // tm, MLA_V, tm), BF16),
        ],
        compiler_params=_cparams(("arbitrary",)),
        name="mla_kv_up",
    )(proj_b, proj_b, g, w, cos_p, sin_p)


def _flash_kernel(qt_ref, k_ref, vt_ref, km_ref, vtm_ref, o_ref, *, nkb, tk, n_meta):
    qt = qt_ref[...]
    tq = qt.shape[1]

    def update(carry, k, vt, mask_rows):
        m, l, acc = carry
        s = jnp.dot(k, qt, preferred_element_type=F32)
        if mask_rows is not None:
            rows = lax.broadcasted_iota(jnp.int32, s.shape, 0)
            s = jnp.where(rows < mask_rows, s, NEG_BIG)
        m_new = jnp.maximum(m, jnp.max(s, axis=0, keepdims=True))
        p = jnp.exp(s - m_new)
        alpha = jnp.exp(m - m_new)
        l = alpha * l + jnp.sum(p, axis=0, keepdims=True)
        acc = alpha * acc + jnp.dot(vt, p.astype(BF16), preferred_element_type=F32)
        return m_new, l, acc

    def body(j, carry):
        start = pl.multiple_of(j * tk, tk)
        return update(carry, k_ref[pl.ds(start, tk), :], vt_ref[j], None)

    init = (jnp.full((1, tq), NEG_BIG, F32), jnp.zeros((1, tq), F32), jnp.zeros((MLA_V, tq), F32))
    carry = lax.fori_loop(0, nkb, body, init)
    m, l, acc = update(carry, km_ref[...], vtm_ref[0], n_meta)
    o_ref[...] = (acc / l).T.astype(o_ref.dtype)


def _flash(qt, k, vt, km, vtm, *, n_meta, tq=512):
    heads, kq, M = qt.shape
    nkb, tk = vt.shape[1], vt.shape[3]
    tq = _tile(M, tq)
    tkm = km.shape[1]
    return pl.pallas_call(
        functools.partial(_flash_kernel, nkb=nkb, tk=tk, n_meta=n_meta),
        grid=(heads, M // tq),
        in_specs=[
            pl.BlockSpec((None, kq, tq), lambda h, i: (h, 0, i)),
            pl.BlockSpec((None, M, kq), lambda h, i: (h, 0, 0)),
            pl.BlockSpec((None, nkb, MLA_V, tk), lambda h, i: (h, 0, 0, 0)),
            pl.BlockSpec((None, tkm, kq), lambda h, i: (h, 0, 0)),
            pl.BlockSpec((None, 1, MLA_V, tkm), lambda h, i: (h, 0, 0, 0)),
        ],
        out_specs=pl.BlockSpec((tq, MLA_V), lambda h, i: (i, h)),
        out_shape=jax.ShapeDtypeStruct((M, heads * MLA_V), BF16),
        compiler_params=_cparams(("arbitrary", "arbitrary")),
        name="mla_flash",
    )(qt, k, vt, km, vtm)


def _gla_consts(direction):
    C = GLA_CHUNK
    r = np.arange(C)
    if direction == "fwd":
        cum = (r[None, :] <= r[:, None])
    else:
        cum = (r[None, :] >= r[:, None])
    eq2 = np.zeros((C, C), np.float32)
    ek2 = np.zeros((C, C), np.float32)
    for t in range(C):
        pos = t % 4
        if direction == "fwd":
            if pos == 2:
                eq2[t, t] = 1
            elif pos == 3:
                eq2[t, t] = 1
                eq2[t, t - 1] = 1
            elif pos == 0:
                ek2[t, t + 1] = 1
        else:
            if pos == 1:
                eq2[t, t] = 1
            elif pos == 0:
                eq2[t, t] = 1
                eq2[t, t + 1] = 1
            elif pos == 3:
                ek2[t, t - 1] = 1
    mats = np.concatenate([cum.astype(np.float32), eq2, ek2], axis=0)
    masks = np.zeros((6, C, LANES), np.float32)
    for li, m in enumerate((4, 2, 1)):
        upper = (r % (2 * m)) >= m
        qrows = upper if direction == "fwd" else ~upper
        masks[2 * li] = np.where(qrows, 0.0, NEG_BIG)[:, None]
        masks[2 * li + 1] = np.where(~qrows, 0.0, NEG_BIG)[:, None]
    return jnp.asarray(mats, BF16), jnp.asarray(masks, F32)


def _split3_dot(mat, x):
    x1 = x.astype(BF16)
    r1 = x - x1.astype(F32)
    x2 = r1.astype(BF16)
    x3 = (r1 - x2.astype(F32)).astype(BF16)
    return (jnp.dot(mat, x1, preferred_element_type=F32)
            + jnp.dot(mat, x2, preferred_element_type=F32)
            + jnp.dot(mat, x3, preferred_element_type=F32))


def _log_decay(lr, w_ref, b_ref):
    z = jnp.dot(lr.astype(BF16), w_ref[...], preferred_element_type=F32) + b_ref[...]
    return jax.nn.log_sigmoid(z) * (1.0 / GLA_GATE_NORM)


def _gla_chunk(q, k, v, g, st_ref, c_ref, mats_ref, masks_ref, *, heads, direction, want_out):
    C = GLA_CHUNK
    fwd = direction == "fwd"
    e_all = _split3_dot(mats_ref[...], g)
    c_all = e_all[0:C]
    eq2_all = e_all[C:2 * C]
    ek2_all = e_all[2 * C:3 * C]
    c_ref[...] = c_all
    far = C - 1 if fwd else 0
    outs = []
    for h in range(heads):
        ls = slice(h * GLA_DK, (h + 1) * GLA_DK)
        vs = slice(h * GLA_DV, (h + 1) * GLA_DV)
        kh, gh, ch = k[:, ls], g[:, ls], c_all[:, ls]
        qh = q[:, ls] if want_out else None
        vh = v[:, vs]
        tot = c_ref[pl.ds(far, 1), ls]
        st = st_ref[h]
        if want_out:
            xor = lax.broadcasted_iota(jnp.int32, (C, C), 0) ^ lax.broadcasted_iota(jnp.int32, (C, C), 1)
            dg = jnp.sum(qh * kh, axis=-1, keepdims=True)
            att = jnp.where(xor == 0, dg, 0.0)
            qcat, kcat = [], []
            for m in GLA_LEVELS:
                if m >= 8:
                    qp, kp = [], []
                    for r0 in range(0, C, 2 * m):
                        lo = slice(r0, r0 + m)
                        hi = slice(r0 + m, r0 + 2 * m)
                        ref = c_ref[pl.ds(r0 + m - 1 if fwd else r0 + m, 1), ls]
                        qr, kr = (hi, lo) if fwd else (lo, hi)
                        qa = qh[qr] * jnp.exp(ch[qr] - ref)
                        ka = kh[kr] * jnp.exp(ref - ch[kr])
                        z = jnp.zeros((m, GLA_DK), F32)
                        qp += [z, qa] if fwd else [qa, z]
                        kp += [ka, z] if fwd else [z, ka]
                    qcat.append(jnp.concatenate(qp, axis=0).astype(BF16))
                    kcat.append(jnp.concatenate(kp, axis=0).astype(BF16))
                else:
                    li = (4, 2, 1).index(m)
                    mq = masks_ref[2 * li]
                    mk = masks_ref[2 * li + 1]
                    if m == 4:
                        ref = jnp.concatenate(
                            [jnp.broadcast_to(c_ref[pl.ds(r0 + 3 if fwd else r0 + 4, 1), ls], (8, GLA_DK))
                             for r0 in range(0, C, 8)], axis=0)
                        eq = ch - ref
                        ek = ref - ch
                    elif m == 2:
                        eq = eq2_all[:, ls]
                        ek = ek2_all[:, ls]
                    else:
                        eq = gh
                        ek = jnp.zeros_like(gh)
                    qcat.append((qh * jnp.exp(eq + mq)).astype(BF16))
                    kcat.append((kh * jnp.exp(ek + mk)).astype(BF16))
                pm = lax.dot_general(qcat.pop(), kcat.pop(), (((1,), (1,)), ((), ())),
                                     preferred_element_type=F32)
                att = att + (pm if 2 * m == C else jnp.where(xor < 2 * m, pm, 0.0))
            qs = (qh * jnp.exp(ch)).astype(BF16)
            o = jnp.dot(att.astype(BF16), vh, preferred_element_type=F32)
            o = o + lax.dot_general(qs, st.astype(BF16), (((1,), (1,)), ((), ())),
                                    preferred_element_type=F32)
            outs.append(o)
        kd = (kh * jnp.exp(tot - ch)).astype(BF16)
        ut = lax.dot_general(vh, kd, (((0,), (0,)), ((), ())), preferred_element_type=F32)
        st_ref[h] = st * jnp.exp(tot) + ut
    return outs if want_out else None


def _gla_fwd_kernel(q_ref, k_ref, v_ref, lr_ref, km_ref, vm_ref, lrm_ref, w_ref, b_ref, mats_ref, masks_ref,
                    o_ref, st_ref, c_ref, *, heads, n_meta):
    n = pl.program_id(0)
    common = dict(heads=heads, direction="fwd")

    @pl.when(n == 0)
    def _():
        st_ref[...] = jnp.zeros_like(st_ref)
        g = _log_decay(lrm_ref[...], w_ref, b_ref)
        rows = lax.broadcasted_iota(jnp.int32, g.shape, 0)
        g = jnp.where(rows < n_meta, g, 0.0)
        k = km_ref[...].astype(F32)
        _gla_chunk(None, k, vm_ref[...], g, st_ref, c_ref, mats_ref, masks_ref, want_out=False, **common)

    @pl.when(n > 0)
    def _():
        g = _log_decay(lr_ref[...], w_ref, b_ref)
        q = q_ref[...].astype(F32) * (GLA_DK ** -0.5)
        k = k_ref[...].astype(F32)
        outs = _gla_chunk(q, k, v_ref[...], g, st_ref, c_ref, mats_ref, masks_ref, want_out=True, **common)
        for h, o in enumerate(outs):
            o_ref[:, h * GLA_DV:(h + 1) * GLA_DV] = o


def _gla_bwd_kernel(q_ref, k_ref, v_ref, gog_ref, lr_ref, of_ref, w_ref, b_ref, gn_ref, mats_ref, masks_ref,
                    o_ref, st_ref, c_ref, *, heads):
    @pl.when(pl.program_id(0) == 0)
    def _():
        st_ref[...] = jnp.zeros_like(st_ref)

    g = _log_decay(lr_ref[...], w_ref, b_ref)
    q = q_ref[...].astype(F32) * (GLA_DK ** -0.5)
    k = k_ref[...].astype(F32)
    outs = _gla_chunk(q, k, v_ref[...], g, st_ref, c_ref, mats_ref, masks_ref,
                      heads=heads, direction="bwd", want_out=True)
    gn = gn_ref[...]
    for h, ob in enumerate(outs):
        vs = slice(h * GLA_DV, (h + 1) * GLA_DV)
        o = _rms(of_ref[:, vs] + ob, gn)
        o_ref[:, vs] = (o * jax.nn.silu(gog_ref[:, vs].astype(F32))).astype(o_ref.dtype)


def _gla(proj_ac, proj_b, km, vm, lrm, wf, bf, wb, bb, gn, *, heads, n_meta):
    M = proj_ac.shape[0]
    C = GLA_CHUNK
    nch = M // C
    qw, vw = heads * GLA_DK, heads * GLA_DV
    const2 = lambda n: (0, 0)
    const3 = lambda n: (0, 0, 0)
    scratch = [pltpu.VMEM((heads, GLA_DV, GLA_DK), F32), pltpu.VMEM((C, qw), F32)]
    mats_f, masks_f = _gla_consts("fwd")
    mats_b, masks_b = _gla_consts("bwd")

    xb = lambda n: jnp.maximum(n - 1, 0)
    o_f = pl.pallas_call(
        functools.partial(_gla_fwd_kernel, heads=heads, n_meta=n_meta),
        grid=(nch + 1,),
        in_specs=[
            pl.BlockSpec((C, qw), lambda n: (xb(n), 0)),
            pl.BlockSpec((C, qw), lambda n: (xb(n), 1)),
            pl.BlockSpec((C, vw), lambda n: (xb(n), 1)),
            pl.BlockSpec((C, LANES), lambda n: (xb(n), 0)),
            pl.BlockSpec(km.shape, const2),
            pl.BlockSpec(vm.shape, const2),
            pl.BlockSpec((C, LANES), const2),
            pl.BlockSpec(wf.shape, const2),
            pl.BlockSpec(bf.shape, const2),
            pl.BlockSpec(mats_f.shape, const2),
            pl.BlockSpec(masks_f.shape, const3),
        ],
        out_specs=pl.BlockSpec((C, vw), lambda n: (xb(n), 0)),
        out_shape=jax.ShapeDtypeStruct((M, vw), F32),
        scratch_shapes=scratch,
        compiler_params=_cparams(("arbitrary",)),
        name="gla_fwd",
    )(proj_ac, proj_ac, proj_ac, proj_b, km, vm, lrm, wf, bf, mats_f, masks_f)

    rb = lambda n: nch - 1 - n
    return pl.pallas_call(
        functools.partial(_gla_bwd_kernel, heads=heads),
        grid=(nch,),
        in_specs=[
            pl.BlockSpec((C, qw), lambda n: (rb(n), 0)),
            pl.BlockSpec((C, qw), lambda n: (rb(n), 1)),
            pl.BlockSpec((C, vw), lambda n: (rb(n), 1)),
            pl.BlockSpec((C, vw), lambda n: (rb(n), 2)),
            pl.BlockSpec((C, LANES), lambda n: (rb(n), 1)),
            pl.BlockSpec((C, vw), lambda n: (rb(n), 0)),
            pl.BlockSpec(wb.shape, const2),
            pl.BlockSpec(bb.shape, const2),
            pl.BlockSpec(gn.shape, const2),
            pl.BlockSpec(mats_b.shape, const2),
            pl.BlockSpec(masks_b.shape, const3),
        ],
        out_specs=pl.BlockSpec((C, vw), lambda n: (rb(n), 0)),
        out_shape=jax.ShapeDtypeStruct((M, vw), BF16),
        scratch_shapes=scratch,
        compiler_params=_cparams(("arbitrary",)),
        name="gla_bwd",
    )(proj_ac, proj_ac, proj_ac, proj_ac, proj_b, o_f, wb, bb, gn, mats_b, masks_b)


def _merge_kernel(og_ref, om_ref, wg_ref, wm_ref, ga_ref, gb_ref, o_ref):
    yg = jnp.dot(og_ref[...], wg_ref[...], preferred_element_type=F32)
    ym = jnp.dot(om_ref[...], wm_ref[...], preferred_element_type=F32)
    ga = jax.nn.sigmoid(ga_ref[...].astype(F32))
    gb = jax.nn.sigmoid(gb_ref[...].astype(F32))
    o_ref[...] = (ga * yg + gb * ym).astype(o_ref.dtype)


def _merge(og, om, wg, wm, proj_ac, *, ga_col0, tm=1024, tn=1024):
    M, D = og.shape[0], wg.shape[1]
    tm, tn = _tile(M, tm), _tile(D, tn)
    ga_blk = ga_col0 // tn
    gb_blk = (ga_col0 + D) // tn
    return pl.pallas_call(
        _merge_kernel,
        grid=(M // tm, D // tn),
        in_specs=[
            pl.BlockSpec((tm, og.shape[1]), lambda i, j: (i, 0)),
            pl.BlockSpec((tm, om.shape[1]), lambda i, j: (i, 0)),
            pl.BlockSpec((wg.shape[0], tn), lambda i, j: (0, j)),
            pl.BlockSpec((wm.shape[0], tn), lambda i, j: (0, j)),
            pl.BlockSpec((tm, tn), lambda i, j: (i, j + ga_blk)),
            pl.BlockSpec((tm, tn), lambda i, j: (i, j + gb_blk)),
        ],
        out_specs=pl.BlockSpec((tm, tn), lambda i, j: (i, j)),
        out_shape=jax.ShapeDtypeStruct((M, D), BF16),
        compiler_params=_cparams(("arbitrary", "arbitrary")),
        name="merge_out_proj",
    )(og, om, wg, wm, proj_ac, proj_ac)


def _rmm_kernel(a_ref, b_ref, r_ref, o_ref, *acc, nk):
    part = jnp.dot(a_ref[...], b_ref[...], preferred_element_type=F32)
    if nk == 1:
        o_ref[...] = r_ref[...] + part
        return
    acc_ref, = acc
    kk = pl.program_id(2)

    @pl.when(kk == 0)
    def _():
        acc_ref[...] = part

    @pl.when(kk > 0)
    def _():
        acc_ref[...] += part

    @pl.when(kk == nk - 1)
    def _():
        o_ref[...] = r_ref[...] + acc_ref[...]


def _res_matmul(a, b, res, *, tm=1024, tn=1024, tk=4096, name):
    M, K = a.shape
    N = b.shape[1]
    tm, tn, tk = _tile(M, tm), _tile(N, tn), _tile(K, tk)
    nk = K // tk
    return pl.pallas_call(
        functools.partial(_rmm_kernel, nk=nk),
        grid=(M // tm, N // tn, nk),
        in_specs=[
            pl.BlockSpec((tm, tk), lambda i, j, k: (i, k)),
            pl.BlockSpec((tk, tn), lambda i, j, k: (k, j)),
            pl.BlockSpec((tm, tn), lambda i, j, k: (i, j)),
        ],
        out_specs=pl.BlockSpec((tm, tn), lambda i, j, k: (i, j)),
        out_shape=jax.ShapeDtypeStruct((M, N), F32),
        scratch_shapes=[pltpu.VMEM((tm, tn), F32)] if nk > 1 else [],
        compiler_params=_cparams(("arbitrary", "arbitrary", "arbitrary")),
        name=name,
    )(a, b, res)


def _final_norm_kernel(h_ref, g_ref, o_ref):
    o_ref[...] = _rms(h_ref[...], g_ref[...])


def _final_norm(h, g, *, tm=256):
    M, D = h.shape
    tm = _tile(M, tm)
    return pl.pallas_call(
        _final_norm_kernel,
        grid=(M // tm,),
        in_specs=[pl.BlockSpec((tm, D), lambda i: (i, 0)), pl.BlockSpec((1, D), lambda i: (0, 0))],
        out_specs=pl.BlockSpec((tm, D), lambda i: (i, 0)),
        out_shape=jax.ShapeDtypeStruct((M, D), F32),
        compiler_params=_cparams(("arbitrary",)),
        name="final_norm",
    )(h, g)


def _pad_cols(w, width):
    return jnp.pad(w, ((0, 0), (0, width - w.shape[1])))


def kernel(x, meta_tokens, ln1, w_in, gla_wf, gla_bf, gla_wb, gla_bb, gla_norm, q_norm, w_uq, kv_norm, w_ukv,
           w_gla_out, w_mla_out, w_o, ln2, w_ff1, w_ff2, final_norm):
    assert x.shape[0] == 1 and ln1.shape[0] == 1, "one sequence, one layer"
    S, D = x.shape[1], x.shape[2]
    n_meta = meta_tokens.shape[0]
    rank = gla_wf.shape[1]
    qk_w = gla_wf.shape[2]
    hg = qk_w // GLA_DK
    v_w = hg * GLA_DV
    q_rank, kv_rank = w_uq.shape[1], w_ukv.shape[1]
    hm = w_uq.shape[2] // (MLA_NOPE + MLA_ROPE)
    half = MLA_ROPE // 2
    assert n_meta <= GLA_CHUNK and rank <= LANES and hm % 2 == 0 and S % GLA_CHUNK == 0

    w = w_in[0]
    o_gog_end = 2 * qk_w + 2 * v_w
    o_lrb = o_gog_end + rank
    o_cq = o_lrb + rank
    o_ckv = o_cq + q_rank
    o_kr = o_ckv + kv_rank
    o_ga = o_kr + MLA_ROPE
    w_ac = jnp.concatenate([w[:, :o_gog_end], w[:, o_ga:]], axis=1).astype(BF16)
    w_b = jnp.concatenate([
        _pad_cols(w[:, o_gog_end:o_lrb], LANES), _pad_cols(w[:, o_lrb:o_cq], LANES),
        _pad_cols(w[:, o_kr:o_kr + half], LANES), _pad_cols(w[:, o_kr + half:o_ga], LANES),
        w[:, o_ckv:o_kr], w[:, o_cq:o_ckv]], axis=1).astype(BF16)
    nb = w_b.shape[1]
    kr_blk = 1
    ckv_blk = (4 * LANES) // kv_rank
    cq_blk = (4 * LANES + kv_rank) // q_rank
    assert (4 * LANES) % kv_rank == 0 and (4 * LANES + kv_rank) % q_rank == 0

    wq = w_uq[0].reshape(q_rank, hm, MLA_NOPE + MLA_ROPE)
    wq = jnp.concatenate([wq[:, :, :MLA_NOPE].reshape(q_rank, -1), wq[:, :, MLA_NOPE:].reshape(q_rank, -1)],
                         axis=1).astype(BF16)
    wkv = w_ukv[0].astype(BF16)
    wf = jnp.pad(gla_wf[0], ((0, LANES - rank), (0, 0))).astype(BF16)
    wb = jnp.pad(gla_wb[0], ((0, LANES - rank), (0, 0))).astype(BF16)
    wgo, wmo, wo = w_gla_out[0].astype(BF16), w_mla_out[0].astype(BF16), w_o[0].astype(BF16)
    w1, w2 = w_ff1[0].astype(BF16), w_ff2[0].astype(BF16)

    inv_freq = ROPE_THETA ** (-jnp.arange(0, MLA_ROPE, 2, dtype=F32) / MLA_ROPE)
    ang = jnp.arange(n_meta + S, dtype=F32)[:, None] * inv_freq[None, :]
    cos, sin = jnp.cos(ang), jnp.sin(ang)
    cos_x, sin_x = cos[n_meta:], sin[n_meta:]
    cos_xp, sin_xp = _pad_cols(cos_x, LANES), _pad_cols(sin_x, LANES)
    mrows = GLA_CHUNK
    padm = lambda a: jnp.pad(a, ((0, mrows - n_meta), (0, LANES - a.shape[1])))
    cos_mp, sin_mp = padm(cos[:n_meta]), padm(sin[:n_meta])

    xs = x[0]
    xm = jnp.pad(meta_tokens.astype(F32), ((0, mrows - n_meta), (0, 0)))
    g1 = ln1[0][None, :]

    proj_ac = _norm_matmul(xs, g1, w_ac, out_dtype=BF16, name="in_proj_ac")
    proj_b = _norm_matmul(xs, g1, w_b, out_dtype=F32, name="in_proj_b")
    proj_b_m = _norm_matmul(xm, g1, w_b, out_dtype=F32, name="in_proj_b_meta")
    kv_m = _norm_matmul(xm, g1, w_ac, out_dtype=BF16, n_out=qk_w + v_w, col_off=1, tn=qk_w,
                        name="in_proj_kv_meta")

    og = _gla(proj_ac, proj_b, kv_m[:, :qk_w], kv_m[:, qk_w:], proj_b_m[:, :LANES],
              wf, gla_bf, wb, gla_bb, gla_norm, heads=hg, n_meta=n_meta)

    qt = _q_up(proj_b, cq_blk, q_norm, wq, cos_x.T, sin_x.T, heads=hm)
    gkv = kv_norm
    k_x, vt_x = _kv_up(proj_b, gkv, wkv, cos_xp, sin_xp, heads=hm, ckv_blk=ckv_blk, kr_blk=kr_blk,
                       tm=_tile(S, 512))
    k_m, vt_m = _kv_up(proj_b_m, gkv, wkv, cos_mp, sin_mp, heads=hm, ckv_blk=ckv_blk, kr_blk=kr_blk, tm=mrows)
    om = _flash(qt, k_x, vt_x, k_m, vt_m, n_meta=n_meta)

    merged = _merge(og, om, wgo, wmo, proj_ac, ga_col0=o_gog_end)
    h1 = _res_matmul(merged, wo, xs, name="o_proj")
    hf = _norm_matmul(h1, ln2[0][None, :], w1, out_dtype=BF16, act="relu2", name="ffn_up")
    h2 = _res_matmul(hf, w2, h1, tk=2048, name="ffn_down")
    return _final_norm(h2, final_norm[None, :])[None]
```

```python
import functools
import math

import jax
import jax.numpy as jnp
import numpy as np
from jax import lax
from jax.experimental import pallas as pl
from jax.experimental.pallas import tpu as pltpu

F32 = jnp.float32
BF16 = jnp.bfloat16

GLA_DK = 128
GLA_DV = 256
GLA_GATE_NORM = 16.0
MLA_NOPE = 128
MLA_ROPE = 64
MLA_V = 128
ROPE_THETA = 10000.0
EPS = 1e-6

LANES = 128
GLA_CHUNK = 128
GLA_LEVELS = (64, 32, 16, 8, 4, 2, 1)
NEG_BIG = -1e30
VMEM_LIMIT = 56 * 1024 * 1024


def _cparams(sem):
    return pltpu.CompilerParams(dimension_semantics=sem, vmem_limit_bytes=VMEM_LIMIT)


def _tile(n, pref):
    t = min(n, pref)
    while n % t:
        t //= 2
    return t


def _rms(a, g, eps=EPS):
    ms = jnp.mean(a * a, axis=-1, keepdims=True)
    return a * lax.rsqrt(ms + eps) * g


def _nmm_kernel(a_ref, g_ref, b_ref, o_ref, an_ref, *, act):
    @pl.when(pl.program_id(1) == 0)
    def _():
        an_ref[...] = _rms(a_ref[...], g_ref[...]).astype(BF16)

    y = jnp.dot(an_ref[...], b_ref[...], preferred_element_type=F32)
    if act == "relu2":
        y = jnp.square(jnp.maximum(y, 0.0))
    o_ref[...] = y.astype(o_ref.dtype)


def _norm_matmul(a, g, b, *, out_dtype, n_out=None, col_off=0, tm=512, tn=1024, act=None, name):
    M, K = a.shape
    n_out = b.shape[1] if n_out is None else n_out
    tm = _tile(M, tm)
    tn = _tile(n_out, tn)
    return pl.pallas_call(
        functools.partial(_nmm_kernel, act=act),
        grid=(M // tm, n_out // tn),
        in_specs=[
            pl.BlockSpec((tm, K), lambda i, j: (i, 0)),
            pl.BlockSpec((1, K), lambda i, j: (0, 0)),
            pl.BlockSpec((K, tn), lambda i, j: (0, j + col_off)),
        ],
        out_specs=pl.BlockSpec((tm, tn), lambda i, j: (i, j)),
        out_shape=jax.ShapeDtypeStruct((M, n_out), out_dtype),
        scratch_shapes=[pltpu.VMEM((tm, K), BF16)],
        compiler_params=_cparams(("arbitrary", "arbitrary")),
        name=name,
    )(a, g, b)


def _q_up_kernel(cq_ref, g_ref, w_ref, cos_ref, sin_ref, qt_ref, *, heads, scale):
    cn = _rms(cq_ref[...], g_ref[...]).astype(BF16)
    y = jnp.dot(cn, w_ref[...], preferred_element_type=F32) * scale
    c = cos_ref[...]
    s = sin_ref[...]
    half = MLA_ROPE // 2
    for h in range(heads):
        qt_ref[h, 0:MLA_NOPE, :] = y[:, h * MLA_NOPE:(h + 1) * MLA_NOPE].T.astype(BF16)
    rope0 = heads * MLA_NOPE
    for p in range(heads // 2):
        blk = y[:, rope0 + p * LANES: rope0 + (p + 1) * LANES].T
        for u in range(2):
            x1 = blk[u * MLA_ROPE: u * MLA_ROPE + half]
            x2 = blk[u * MLA_ROPE + half: (u + 1) * MLA_ROPE]
            h = 2 * p + u
            qt_ref[h, MLA_NOPE:MLA_NOPE + half, :] = (x1 * c - x2 * s).astype(BF16)
            qt_ref[h, MLA_NOPE + half:MLA_NOPE + MLA_ROPE, :] = (x1 * s + x2 * c).astype(BF16)
    pad = qt_ref.shape[1] - MLA_NOPE - MLA_ROPE
    for h in range(heads):
        qt_ref[h, MLA_NOPE + MLA_ROPE:, :] = jnp.zeros((pad, qt_ref.shape[2]), BF16)


def _q_up(proj_b, cq_blk, g, w, cos_t, sin_t, *, heads, tm=512):
    M = proj_b.shape[0]
    R = g.shape[1]
    tm = _tile(M, tm)
    kq = 2 * LANES
    return pl.pallas_call(
        functools.partial(_q_up_kernel, heads=heads, scale=float((MLA_NOPE + MLA_ROPE) ** -0.5 * math.log2(math.e))),
        grid=(M // tm,),
        in_specs=[
            pl.BlockSpec((tm, R), lambda i: (i, cq_blk)),
            pl.BlockSpec((1, R), lambda i: (0, 0)),
            pl.BlockSpec(w.shape, lambda i: (0, 0)),
            pl.BlockSpec((MLA_ROPE // 2, tm), lambda i: (0, i)),
            pl.BlockSpec((MLA_ROPE // 2, tm), lambda i: (0, i)),
        ],
        out_specs=pl.BlockSpec((heads, kq, tm), lambda i: (0, 0, i)),
        out_shape=jax.ShapeDtypeStruct((heads, kq, M), BF16),
        compiler_params=_cparams(("arbitrary",)),
        name="mla_q_up",
    )(proj_b, g, w, cos_t, sin_t)


def _kv_up_kernel(ckv_ref, kr_ref, g_ref, w_ref, cos_ref, sin_ref, k_ref, vt_ref, *, heads):
    cn = _rms(ckv_ref[...], g_ref[...]).astype(BF16)
    y = jnp.dot(cn, w_ref[...], preferred_element_type=F32)
    x1 = kr_ref[:, 0:LANES]
    x2 = kr_ref[:, LANES:2 * LANES]
    c = cos_ref[...]
    s = sin_ref[...]
    kpe = (x1 * c - x2 * s) + pltpu.roll(x1 * s + x2 * c, MLA_ROPE // 2, axis=1)
    kpe = kpe.astype(BF16)
    hw = MLA_NOPE + MLA_V
    for h in range(heads):
        k_ref[h, :, 0:MLA_NOPE] = y[:, h * hw: h * hw + MLA_NOPE].astype(BF16)
        k_ref[h, :, MLA_NOPE:] = kpe
        vt_ref[h, 0] = y[:, h * hw + MLA_NOPE:(h + 1) * hw].T.astype(BF16)


def _kv_up(proj_b, g, w, cos_p, sin_p, *, heads, ckv_blk, kr_blk, tm):
    M = proj_b.shape[0]
    R = g.shape[1]
    return pl.pallas_call(
        functools.partial(_kv_up_kernel, heads=heads),
        grid=(M // tm,),
        in_specs=[
            pl.BlockSpec((tm, R), lambda i: (i, ckv_blk)),
            pl.BlockSpec((tm, 2 * LANES), lambda i: (i, kr_blk)),
            pl.BlockSpec((1, R), lambda i: (0, 0)),
            pl.BlockSpec(w.shape, lambda i: (0, 0)),
            pl.BlockSpec((tm, LANES), lambda i: (i, 0)),
            pl.BlockSpec((tm, LANES), lambda i: (i, 0)),
        ],
        out_specs=[
            pl.BlockSpec((heads, tm, 2 * LANES), lambda i: (0, i, 0)),
            pl.BlockSpec((heads, 1, MLA_V, tm), lambda i: (0, i, 0, 0)),
        ],
        out_shape=[
            jax.ShapeDtypeStruct((heads, M, 2 * LANES), BF16),
            jax.ShapeDtypeStruct((heads, M // tm, MLA_V, tm), BF16),
        ],
        compiler_params=_cparams(("arbitrary",)),
        name="mla_kv_up",
    )(proj_b, proj_b, g, w, cos_p, sin_p)


def _flash_kernel(qt_ref, k_ref, vt_ref, km_ref, vtm_ref, o_ref, s_ref, p_ref, acc_ref, *, nkb, tk, n_meta):
    tq = qt_ref.shape[1]
    strip = min(tk, 64)

    def scores(j, slot):
        start = pl.multiple_of(j * tk, tk)
        s_ref[slot] = jnp.dot(k_ref[pl.ds(start, tk), :], qt_ref[...], preferred_element_type=F32)

    def softmax(slot, m, l):
        m_new = jnp.maximum(m, jnp.max(s_ref[slot], axis=0, keepdims=True))
        psum = jnp.zeros((8, tq), F32)
        for r in range(0, tk, strip):
            p = jnp.exp2(s_ref[slot, r:r + strip, :] - m_new)
            psum = psum + jnp.sum(p.reshape(strip // 8, 8, tq), axis=0)
            p_ref[slot, r:r + strip, :] = p.astype(BF16)
        alpha = jnp.exp2(m - m_new)
        return m_new, alpha * l + jnp.sum(psum, axis=0, keepdims=True), alpha

    def values(j, slot, alpha):
        acc_ref[...] = alpha * acc_ref[...] + jnp.dot(vt_ref[j], p_ref[slot], preferred_element_type=F32)

    scores(0, 0)
    s_m = jnp.dot(km_ref[...], qt_ref[...], preferred_element_type=F32)
    rows = lax.broadcasted_iota(jnp.int32, s_m.shape, 0)
    s_m = jnp.where(rows < n_meta, s_m, NEG_BIG)
    m = jnp.max(s_m, axis=0, keepdims=True)
    p_m = jnp.exp2(s_m - m)
    l = jnp.sum(p_m, axis=0, keepdims=True)
    acc_ref[...] = jnp.dot(vtm_ref[0], p_m.astype(BF16), preferred_element_type=F32)

    if nkb > 1:
        scores(1, 1)
    m, l, alpha = softmax(0, m, l)

    def pair(jj, carry):
        m, l, alpha = carry
        for u in range(2):
            j = 1 + 2 * jj + u
            scores(j + 1, u)
            values(j - 1, u, alpha)
            m, l, alpha = softmax(1 - u, m, l)
        return m, l, alpha

    n_pairs = 0
    m, l, alpha = lax.fori_loop(0, n_pairs, pair, (m, l, alpha))
    for j in range(1 + 2 * n_pairs, nkb):
        if j + 1 < nkb:
            scores(j + 1, (j + 1) % 2)
        values(j - 1, (j - 1) % 2, alpha)
        m, l, alpha = softmax(j % 2, m, l)
    values(nkb - 1, (nkb - 1) % 2, alpha)
    o_ref[...] = (acc_ref[...] / l).T.astype(o_ref.dtype)


def _flash(qt, k, vt, km, vtm, *, n_meta, tq=512):
    heads, kq, M = qt.shape
    nkb, tk = vt.shape[1], vt.shape[3]
    tq = _tile(M, tq)
    tkm = km.shape[1]
    return pl.pallas_call(
        functools.partial(_flash_kernel, nkb=nkb, tk=tk, n_meta=n_meta),
        grid=(heads, M // tq),
        in_specs=[
            pl.BlockSpec((None, kq, tq), lambda h, i: (h, 0, i)),
            pl.BlockSpec((None, M, kq), lambda h, i: (h, 0, 0)),
            pl.BlockSpec((None, nkb, MLA_V, tk), lambda h, i: (h, 0, 0, 0)),
            pl.BlockSpec((None, tkm, kq), lambda h, i: (h, 0, 0)),
            pl.BlockSpec((None, 1, MLA_V, tkm), lambda h, i: (h, 0, 0, 0)),
        ],
        out_specs=pl.BlockSpec((tq, MLA_V), lambda h, i: (i, h)),
        out_shape=jax.ShapeDtypeStruct((M, heads * MLA_V), BF16),
        scratch_shapes=[pltpu.VMEM((2, tk, tq), F32), pltpu.VMEM((2, tk, tq), BF16), pltpu.VMEM((MLA_V, tq), F32)],
        compiler_params=_cparams(("arbitrary", "arbitrary")),
        name="mla_flash",
    )(qt, k, vt, km, vtm)


def _gla_consts(direction):
    C = GLA_CHUNK
    r = np.arange(C)
    if direction == "fwd":
        cum = (r[None, :] <= r[:, None])
    else:
        cum = (r[None, :] >= r[:, None])
    eq2 = np.zeros((C, C), np.float32)
    ek2 = np.zeros((C, C), np.float32)
    for t in range(C):
        pos = t % 4
        if direction == "fwd":
            if pos == 2:
                eq2[t, t] = 1
            elif pos == 3:
                eq2[t, t] = 1
                eq2[t, t - 1] = 1
            elif pos == 0:
                ek2[t, t + 1] = 1
        else:
            if pos == 1:
                eq2[t, t] = 1
            elif pos == 0:
                eq2[t, t] = 1
                eq2[t, t + 1] = 1
            elif pos == 3:
                ek2[t, t - 1] = 1
    mats = np.concatenate([cum.astype(np.float32), eq2, ek2], axis=0)
    masks = np.zeros((6, C, LANES), np.float32)
    for li, m in enumerate((4, 2, 1)):
        upper = (r % (2 * m)) >= m
        qrows = upper if direction == "fwd" else ~upper
        masks[2 * li] = np.where(qrows, 0.0, NEG_BIG)[:, None]
        masks[2 * li + 1] = np.where(~qrows, 0.0, NEG_BIG)[:, None]
    return jnp.asarray(mats, BF16), jnp.asarray(masks, F32)


def _split3_dot(mat, x):
    x1 = x.astype(BF16)
    r1 = x - x1.astype(F32)
    x2 = r1.astype(BF16)
    x3 = (r1 - x2.astype(F32)).astype(BF16)
    return (jnp.dot(mat, x1, preferred_element_type=F32)
            + jnp.dot(mat, x2, preferred_element_type=F32)
            + jnp.dot(mat, x3, preferred_element_type=F32))


def _log_decay(lr, w_ref, b_ref):
    z = jnp.dot(lr.astype(BF16), w_ref[...], preferred_element_type=F32) + b_ref[...]
    return jax.nn.log_sigmoid(z) * (1.0 / GLA_GATE_NORM)


def _gla_chunk(q, k, v, g, st_ref, c_ref, mats_ref, masks_ref, *, heads, direction, want_out):
    C = GLA_CHUNK
    fwd = direction == "fwd"
    e_all = _split3_dot(mats_ref[...], g)
    c_all = e_all[0:C]
    eq2_all = e_all[C:2 * C]
    ek2_all = e_all[2 * C:3 * C]
    c_ref[...] = c_all
    far = C - 1 if fwd else 0
    outs = []
    for h in range(heads):
        ls = slice(h * GLA_DK, (h + 1) * GLA_DK)
        vs = slice(h * GLA_DV, (h + 1) * GLA_DV)
        kh, gh, ch = k[:, ls], g[:, ls], c_all[:, ls]
        qh = q[:, ls] if want_out else None
        vh = v[:, vs]
        tot = c_ref[pl.ds(far, 1), ls]
        st = st_ref[h]
        if want_out:
            xor = lax.broadcasted_iota(jnp.int32, (C, C), 0) ^ lax.broadcasted_iota(jnp.int32, (C, C), 1)
            dg = jnp.sum(qh * kh, axis=-1, keepdims=True)
            att = jnp.where(xor == 0, dg, 0.0)
            qcat, kcat = [], []
            for m in GLA_LEVELS:
                if m >= 8:
                    qp, kp = [], []
                    for r0 in range(0, C, 2 * m):
                        lo = slice(r0, r0 + m)
                        hi = slice(r0 + m, r0 + 2 * m)
                        ref = c_ref[pl.ds(r0 + m - 1 if fwd else r0 + m, 1), ls]
                        qr, kr = (hi, lo) if fwd else (lo, hi)
                        qa = qh[qr] * jnp.exp(ch[qr] - ref)
                        ka = kh[kr] * jnp.exp(ref - ch[kr])
                        z = jnp.zeros((m, GLA_DK), F32)
                        qp += [z, qa] if fwd else [qa, z]
                        kp += [ka, z] if fwd else [z, ka]
                    qcat.append(jnp.concatenate(qp, axis=0).astype(BF16))
                    kcat.append(jnp.concatenate(kp, axis=0).astype(BF16))
                else:
                    li = (4, 2, 1).index(m)
                    mq = masks_ref[2 * li]
                    mk = masks_ref[2 * li + 1]
                    if m == 4:
                        ref = jnp.concatenate(
                            [jnp.broadcast_to(c_ref[pl.ds(r0 + 3 if fwd else r0 + 4, 1), ls], (8, GLA_DK))
                             for r0 in range(0, C, 8)], axis=0)
                        eq = ch - ref
                        ek = ref - ch
                    elif m == 2:
                        eq = eq2_all[:, ls]
                        ek = ek2_all[:, ls]
                    else:
                        eq = gh
                        ek = jnp.zeros_like(gh)
                    qcat.append((qh * jnp.exp(eq + mq)).astype(BF16))
                    kcat.append((kh * jnp.exp(ek + mk)).astype(BF16))
                pm = lax.dot_general(qcat.pop(), kcat.pop(), (((1,), (1,)), ((), ())),
                                     preferred_element_type=F32)
                att = att + (pm if 2 * m == C else jnp.where(xor < 2 * m, pm, 0.0))
            qs = (qh * jnp.exp(ch)).astype(BF16)
            o = jnp.dot(att.astype(BF16), vh, preferred_element_type=F32)
            o = o + lax.dot_general(qs, st.astype(BF16), (((1,), (1,)), ((), ())),
                                    preferred_element_type=F32)
            outs.append(o)
        kd = (kh * jnp.exp(tot - ch)).astype(BF16)
        ut = lax.dot_general(vh, kd, (((0,), (0,)), ((), ())), preferred_element_type=F32)
        st_ref[h] = st * jnp.exp(tot) + ut
    return outs if want_out else None


def _gla_fwd_kernel(q_ref, k_ref, v_ref, lr_ref, km_ref, vm_ref, lrm_ref, w_ref, b_ref, mats_ref, masks_ref,
                    o_ref, st_ref, c_ref, *, heads, n_meta):
    n = pl.program_id(0)
    common = dict(heads=heads, direction="fwd")

    @pl.when(n == 0)
    def _():
        st_ref[...] = jnp.zeros_like(st_ref)
        g = _log_decay(lrm_ref[...], w_ref, b_ref)
        rows = lax.broadcasted_iota(jnp.int32, g.shape, 0)
        g = jnp.where(rows < n_meta, g, 0.0)
        k = km_ref[...].astype(F32)
        _gla_chunk(None, k, vm_ref[...], g, st_ref, c_ref, mats_ref, masks_ref, want_out=False, **common)

    @pl.when(n > 0)
    def _():
        g = _log_decay(lr_ref[...], w_ref, b_ref)
        q = q_ref[...].astype(F32) * (GLA_DK ** -0.5)
        k = k_ref[...].astype(F32)
        outs = _gla_chunk(q, k, v_ref[...], g, st_ref, c_ref, mats_ref, masks_ref, want_out=True, **common)
        for h, o in enumerate(outs):
            o_ref[:, h * GLA_DV:(h + 1) * GLA_DV] = o


def _gla_bwd_kernel(q_ref, k_ref, v_ref, gog_ref, lr_ref, of_ref, w_ref, b_ref, gn_ref, mats_ref, masks_ref,
                    o_ref, st_ref, c_ref, *, heads):
    @pl.when(pl.program_id(0) == 0)
    def _():
        st_ref[...] = jnp.zeros_like(st_ref)

    g = _log_decay(lr_ref[...], w_ref, b_ref)
    q = q_ref[...].astype(F32) * (GLA_DK ** -0.5)
    k = k_ref[...].astype(F32)
    outs = _gla_chunk(q, k, v_ref[...], g, st_ref, c_ref, mats_ref, masks_ref,
                      heads=heads, direction="bwd", want_out=True)
    gn = gn_ref[...]
    for h, ob in enumerate(outs):
        vs = slice(h * GLA_DV, (h + 1) * GLA_DV)
        o = _rms(of_ref[:, vs] + ob, gn)
        o_ref[:, vs] = (o * jax.nn.silu(gog_ref[:, vs].astype(F32))).astype(o_ref.dtype)


def _gla(proj_ac, proj_b, km, vm, lrm, wf, bf, wb, bb, gn, *, heads, n_meta):
    M = proj_ac.shape[0]
    C = GLA_CHUNK
    nch = M // C
    qw, vw = heads * GLA_DK, heads * GLA_DV
    const2 = lambda n: (0, 0)
    const3 = lambda n: (0, 0, 0)
    scratch = [pltpu.VMEM((heads, GLA_DV, GLA_DK), F32), pltpu.VMEM((C, qw), F32)]
    mats_f, masks_f = _gla_consts("fwd")
    mats_b, masks_b = _gla_consts("bwd")

    xb = lambda n: jnp.maximum(n - 1, 0)
    o_f = pl.pallas_call(
        functools.partial(_gla_fwd_kernel, heads=heads, n_meta=n_meta),
        grid=(nch + 1,),
        in_specs=[
            pl.BlockSpec((C, qw), lambda n: (xb(n), 0)),
            pl.BlockSpec((C, qw), lambda n: (xb(n), 1)),
            pl.BlockSpec((C, vw), lambda n: (xb(n), 1)),
            pl.BlockSpec((C, LANES), lambda n: (xb(n), 0)),
            pl.BlockSpec(km.shape, const2),
            pl.BlockSpec(vm.shape, const2),
            pl.BlockSpec((C, LANES), const2),
            pl.BlockSpec(wf.shape, const2),
            pl.BlockSpec(bf.shape, const2),
            pl.BlockSpec(mats_f.shape, const2),
            pl.BlockSpec(masks_f.shape, const3),
        ],
        out_specs=pl.BlockSpec((C, vw), lambda n: (xb(n), 0)),
        out_shape=jax.ShapeDtypeStruct((M, vw), F32),
        scratch_shapes=scratch,
        compiler_params=_cparams(("arbitrary",)),
        name="gla_fwd",
    )(proj_ac, proj_ac, proj_ac, proj_b, km, vm, lrm, wf, bf, mats_f, masks_f)

    rb = lambda n: nch - 1 - n
    return pl.pallas_call(
        functools.partial(_gla_bwd_kernel, heads=heads),
        grid=(nch,),
        in_specs=[
            pl.BlockSpec((C, qw), lambda n: (rb(n), 0)),
            pl.BlockSpec((C, qw), lambda n: (rb(n), 1)),
            pl.BlockSpec((C, vw), lambda n: (rb(n), 1)),
            pl.BlockSpec((C, vw), lambda n: (rb(n), 2)),
            pl.BlockSpec((C, LANES), lambda n: (rb(n), 1)),
            pl.BlockSpec((C, vw), lambda n: (rb(n), 0)),
            pl.BlockSpec(wb.shape, const2),
            pl.BlockSpec(bb.shape, const2),
            pl.BlockSpec(gn.shape, const2),
            pl.BlockSpec(mats_b.shape, const2),
            pl.BlockSpec(masks_b.shape, const3),
        ],
        out_specs=pl.BlockSpec((C, vw), lambda n: (rb(n), 0)),
        out_shape=jax.ShapeDtypeStruct((M, vw), BF16),
        scratch_shapes=scratch,
        compiler_params=_cparams(("arbitrary",)),
        name="gla_bwd",
    )(proj_ac, proj_ac, proj_ac, proj_ac, proj_b, o_f, wb, bb, gn, mats_b, masks_b)


def _merge_kernel(og_ref, om_ref, wg_ref, wm_ref, ga_ref, gb_ref, o_ref):
    yg = jnp.dot(og_ref[...], wg_ref[...], preferred_element_type=F32)
    ym = jnp.dot(om_ref[...], wm_ref[...], preferred_element_type=F32)
    ga = jax.nn.sigmoid(ga_ref[...].astype(F32))
    gb = jax.nn.sigmoid(gb_ref[...].astype(F32))
    o_ref[...] = (ga * yg + gb * ym).astype(o_ref.dtype)


def _merge(og, om, wg, wm, proj_ac, *, ga_col0, tm=1024, tn=1024):
    M, D = og.shape[0], wg.shape[1]
    tm, tn = _tile(M, tm), _tile(D, tn)
    ga_blk = ga_col0 // tn
    gb_blk = (ga_col0 + D) // tn
    return pl.pallas_call(
        _merge_kernel,
        grid=(M // tm, D // tn),
        in_specs=[
            pl.BlockSpec((tm, og.shape[1]), lambda i, j: (i, 0)),
            pl.BlockSpec((tm, om.shape[1]), lambda i, j: (i, 0)),
            pl.BlockSpec((wg.shape[0], tn), lambda i, j: (0, j)),
            pl.BlockSpec((wm.shape[0], tn), lambda i, j: (0, j)),
            pl.BlockSpec((tm, tn), lambda i, j: (i, j + ga_blk)),
            pl.BlockSpec((tm, tn), lambda i, j: (i, j + gb_blk)),
        ],
        out_specs=pl.BlockSpec((tm, tn), lambda i, j: (i, j)),
        out_shape=jax.ShapeDtypeStruct((M, D), BF16),
        compiler_params=_cparams(("arbitrary", "arbitrary")),
        name="merge_out_proj",
    )(og, om, wg, wm, proj_ac, proj_ac)


def _rmm_kernel(a_ref, b_ref, r_ref, o_ref, *acc, nk):
    part = jnp.dot(a_ref[...], b_ref[...], preferred_element_type=F32)
    if nk == 1:
        o_ref[...] = r_ref[...] + part
        return
    acc_ref, = acc
    kk = pl.program_id(2)

    @pl.when(kk == 0)
    def _():
        acc_ref[...] = part

    @pl.when(kk > 0)
    def _():
        acc_ref[...] += part

    @pl.when(kk == nk - 1)
    def _():
        o_ref[...] = r_ref[...] + acc_ref[...]


def _res_matmul(a, b, res, *, tm=1024, tn=1024, tk=4096, name):
    M, K = a.shape
    N = b.shape[1]
    tm, tn, tk = _tile(M, tm), _tile(N, tn), _tile(K, tk)
    nk = K // tk
    return pl.pallas_call(
        functools.partial(_rmm_kernel, nk=nk),
        grid=(M // tm, N // tn, nk),
        in_specs=[
            pl.BlockSpec((tm, tk), lambda i, j, k: (i, k)),
            pl.BlockSpec((tk, tn), lambda i, j, k: (k, j)),
            pl.BlockSpec((tm, tn), lambda i, j, k: (i, j)),
        ],
        out_specs=pl.BlockSpec((tm, tn), lambda i, j, k: (i, j)),
        out_shape=jax.ShapeDtypeStruct((M, N), F32),
        scratch_shapes=[pltpu.VMEM((tm, tn), F32)] if nk > 1 else [],
        compiler_params=_cparams(("arbitrary", "arbitrary", "arbitrary")),
        name=name,
    )(a, b, res)


def _final_norm_kernel(h_ref, g_ref, o_ref):
    o_ref[...] = _rms(h_ref[...], g_ref[...])


def _final_norm(h, g, *, tm=256):
    M, D = h.shape
    tm = _tile(M, tm)
    return pl.pallas_call(
        _final_norm_kernel,
        grid=(M // tm,),
        in_specs=[pl.BlockSpec((tm, D), lambda i: (i, 0)), pl.BlockSpec((1, D), lambda i: (0, 0))],
        out_specs=pl.BlockSpec((tm, D), lambda i: (i, 0)),
        out_shape=jax.ShapeDtypeStruct((M, D), F32),
        compiler_params=_cparams(("arbitrary",)),
        name="final_norm",
    )(h, g)


def _pad_cols(w, width):
    return jnp.pad(w, ((0, 0), (0, width - w.shape[1])))


def kernel(x, meta_tokens, ln1, w_in, gla_wf, gla_bf, gla_wb, gla_bb, gla_norm, q_norm, w_uq, kv_norm, w_ukv,
           w_gla_out, w_mla_out, w_o, ln2, w_ff1, w_ff2, final_norm):
    assert x.shape[0] == 1 and ln1.shape[0] == 1, "one sequence, one layer"
    S, D = x.shape[1], x.shape[2]
    n_meta = meta_tokens.shape[0]
    rank = gla_wf.shape[1]
    qk_w = gla_wf.shape[2]
    hg = qk_w // GLA_DK
    v_w = hg * GLA_DV
    q_rank, kv_rank = w_uq.shape[1], w_ukv.shape[1]
    hm = w_uq.shape[2] // (MLA_NOPE + MLA_ROPE)
    half = MLA_ROPE // 2
    assert n_meta <= GLA_CHUNK and rank <= LANES and hm % 2 == 0 and S % GLA_CHUNK == 0

    w = w_in[0]
    o_gog_end = 2 * qk_w + 2 * v_w
    o_lrb = o_gog_end + rank
    o_cq = o_lrb + rank
    o_ckv = o_cq + q_rank
    o_kr = o_ckv + kv_rank
    o_ga = o_kr + MLA_ROPE
    w_ac = jnp.concatenate([w[:, :o_gog_end], w[:, o_ga:]], axis=1).astype(BF16)
    w_b = jnp.concatenate([
        _pad_cols(w[:, o_gog_end:o_lrb], LANES), _pad_cols(w[:, o_lrb:o_cq], LANES),
        _pad_cols(w[:, o_kr:o_kr + half], LANES), _pad_cols(w[:, o_kr + half:o_ga], LANES),
        w[:, o_ckv:o_kr], w[:, o_cq:o_ckv]], axis=1).astype(BF16)
    nb = w_b.shape[1]
    kr_blk = 1
    ckv_blk = (4 * LANES) // kv_rank
    cq_blk = (4 * LANES + kv_rank) // q_rank
    assert (4 * LANES) % kv_rank == 0 and (4 * LANES + kv_rank) % q_rank == 0

    wq = w_uq[0].reshape(q_rank, hm, MLA_NOPE + MLA_ROPE)
    wq = jnp.concatenate([wq[:, :, :MLA_NOPE].reshape(q_rank, -1), wq[:, :, MLA_NOPE:].reshape(q_rank, -1)],
                         axis=1).astype(BF16)
    wkv = w_ukv[0].astype(BF16)
    wf = jnp.pad(gla_wf[0], ((0, LANES - rank), (0, 0))).astype(BF16)
    wb = jnp.pad(gla_wb[0], ((0, LANES - rank), (0, 0))).astype(BF16)
    wgo, wmo, wo = w_gla_out[0].astype(BF16), w_mla_out[0].astype(BF16), w_o[0].astype(BF16)
    w1, w2 = w_ff1[0].astype(BF16), w_ff2[0].astype(BF16)

    inv_freq = ROPE_THETA ** (-jnp.arange(0, MLA_ROPE, 2, dtype=F32) / MLA_ROPE)
    ang = jnp.arange(n_meta + S, dtype=F32)[:, None] * inv_freq[None, :]
    cos, sin = jnp.cos(ang), jnp.sin(ang)
    cos_x, sin_x = cos[n_meta:], sin[n_meta:]
    cos_xp, sin_xp = _pad_cols(cos_x, LANES), _pad_cols(sin_x, LANES)
    mrows = GLA_CHUNK
    padm = lambda a: jnp.pad(a, ((0, mrows - n_meta), (0, LANES - a.shape[1])))
    cos_mp, sin_mp = padm(cos[:n_meta]), padm(sin[:n_meta])

    xs = x[0]
    xm = jnp.pad(meta_tokens.astype(F32), ((0, mrows - n_meta), (0, 0)))
    g1 = ln1[0][None, :]

    proj_ac = _norm_matmul(xs, g1, w_ac, out_dtype=BF16, name="in_proj_ac")
    proj_b = _norm_matmul(xs, g1, w_b, out_dtype=F32, name="in_proj_b")
    proj_b_m = _norm_matmul(xm, g1, w_b, out_dtype=F32, name="in_proj_b_meta")
    kv_m = _norm_matmul(xm, g1, w_ac, out_dtype=BF16, n_out=qk_w + v_w, col_off=1, tn=qk_w,
                        name="in_proj_kv_meta")

    og = _gla(proj_ac, proj_b, kv_m[:, :qk_w], kv_m[:, qk_w:], proj_b_m[:, :LANES],
              wf, gla_bf, wb, gla_bb, gla_norm, heads=hg, n_meta=n_meta)

    qt = _q_up(proj_b, cq_blk, q_norm, wq, cos_x.T, sin_x.T, heads=hm)
    gkv = kv_norm
    k_x, vt_x = _kv_up(proj_b, gkv, wkv, cos_xp, sin_xp, heads=hm, ckv_blk=ckv_blk, kr_blk=kr_blk,
                       tm=_tile(S, 512))
    k_m, vt_m = _kv_up(proj_b_m, gkv, wkv, cos_mp, sin_mp, heads=hm, ckv_blk=ckv_blk, kr_blk=kr_blk, tm=mrows)
    om = _flash(qt, k_x, vt_x, k_m, vt_m, n_meta=n_meta)

    merged = _merge(og, om, wgo, wmo, proj_ac, ga_col0=o_gog_end)
    h1 = _res_matmul(merged, wo, xs, name="o_proj")
    hf = _norm_matmul(h1, ln2[0][None, :], w1, out_dtype=BF16, act="relu2", name="ffn_up")
    h2 = _res_matmul(hf, w2, h1, tk=2048, name="ffn_down")
    return _final_norm(h2, final_norm[None, :])[None]
```

```python
import functools
import math

import jax
import jax.numpy as jnp
import numpy as np
from jax import lax
from jax.experimental import pallas as pl
from jax.experimental.pallas import tpu as pltpu

F32 = jnp.float32
BF16 = jnp.bfloat16

GLA_DK = 128
GLA_DV = 256
GLA_GATE_NORM = 16.0
MLA_NOPE = 128
MLA_ROPE = 64
MLA_V = 128
ROPE_THETA = 10000.0
EPS = 1e-6

LANES = 128
GLA_CHUNK = 128
GLA_LEVELS = (64, 32, 16, 8, 4, 2, 1)
NEG_BIG = -1e30
VMEM_LIMIT = 56 * 1024 * 1024


def _cparams(sem):
    return pltpu.CompilerParams(dimension_semantics=sem, vmem_limit_bytes=VMEM_LIMIT)


def _tile(n, pref):
    t = min(n, pref)
    while n % t:
        t //= 2
    return t


def _rms(a, g, eps=EPS):
    ms = jnp.mean(a * a, axis=-1, keepdims=True)
    return a * lax.rsqrt(ms + eps) * g


def _nmm_kernel(a_ref, g_ref, b_ref, ci_ref, o_ref, co_ref, an_ref, *, act):
    @pl.when(pl.program_id(1) == 0)
    def _():
        an_ref[...] = _rms(a_ref[...], g_ref[...]).astype(BF16)

    y = jnp.dot(an_ref[...], b_ref[...], preferred_element_type=F32)
    if act == "relu2":
        y = jnp.square(jnp.maximum(y, 0.0))
    o_ref[...] = y.astype(o_ref.dtype)
    co_ref[...] = ci_ref[...].astype(BF16)


def _norm_matmul(a, g, b, cast, *, out_dtype, tm=512, tn=1024, act=None, name):
    M, K = a.shape
    N = b.shape[1]
    tm, tn = _tile(M, tm), _tile(N, tn)
    ni, nj = M // tm, N // tn
    nblk = _cast_blocks(cast.shape[0], ni * nj)
    cast_spec = pl.BlockSpec((cast.shape[0] // nblk, cast.shape[1]),
                             lambda i, j: (jnp.minimum(i * nj + j, nblk - 1), 0))
    return pl.pallas_call(
        functools.partial(_nmm_kernel, act=act),
        grid=(ni, nj),
        in_specs=[
            pl.BlockSpec((tm, K), lambda i, j: (i, 0)),
            pl.BlockSpec((1, K), lambda i, j: (0, 0)),
            pl.BlockSpec((K, tn), lambda i, j: (0, j)),
            cast_spec,
        ],
        out_specs=[pl.BlockSpec((tm, tn), lambda i, j: (i, j)), cast_spec],
        out_shape=[jax.ShapeDtypeStruct((M, N), out_dtype), jax.ShapeDtypeStruct(cast.shape, BF16)],
        scratch_shapes=[pltpu.VMEM((tm, K), BF16)],
        compiler_params=_cparams(("arbitrary", "arbitrary")),
        name=name,
    )(a, g, b, cast)


def _rms_rows_kernel(x_ref, g_ref, o_ref):
    o_ref[...] = _rms(x_ref[...], g_ref[...]).astype(o_ref.dtype)


def _rms_rows(x, g, *, out_dtype, tm=256, name):
    M, D = x.shape
    tm = _tile(M, tm)
    return pl.pallas_call(
        _rms_rows_kernel,
        grid=(M // tm,),
        in_specs=[pl.BlockSpec((tm, D), lambda i: (i, 0)), pl.BlockSpec((1, D), lambda i: (0, 0))],
        out_specs=pl.BlockSpec((tm, D), lambda i: (i, 0)),
        out_shape=jax.ShapeDtypeStruct((M, D), out_dtype),
        compiler_params=_cparams(("arbitrary",)),
        name=name,
    )(x, g)


def _ws_kernel(*refs, shift, n_cast):
    a_ref, w_ref = refs[0], refs[1]
    n_in = 2 + (shift is not None) + n_cast
    w2_ref = refs[2] if shift is not None else None
    cast_in = refs[n_in - n_cast:n_in]
    o_ref = refs[n_in]
    cast_out = refs[n_in + 1:n_in + 1 + n_cast]
    wb_ref = refs[n_in + 1 + n_cast]
    K, tn = wb_ref.shape

    @pl.when(pl.program_id(1) == 0)
    def _():
        if shift is None:
            wb_ref[...] = w_ref[...].astype(BF16)
        else:
            rc = _tile(K, 512)
            for r in range(0, K, rc):
                cat = jnp.concatenate([w_ref[r:r + rc, :], w2_ref[r:r + rc, :]], axis=1)
                wb_ref[r:r + rc, :] = cat[:, shift:shift + tn].astype(BF16)

    o_ref[...] = jnp.dot(a_ref[...], wb_ref[...], preferred_element_type=F32).astype(o_ref.dtype)
    for ci, co in zip(cast_in, cast_out):
        co[...] = ci[...].astype(BF16)


def _cast_blocks(rows, steps):
    d = max(1, min(rows // 16, steps))
    while (rows // 16) % d:
        d -= 1
    return d


def _ws_matmul(a, w, *, col0, n_out, out_dtype, tm, tn, casts=(), name):
    M, K = a.shape
    shift = col0 % LANES
    base = col0 - shift
    tm, tn = _tile(M, tm), _tile(math.gcd(n_out, base), tn)
    nj, ni = n_out // tn, M // tm
    in_specs = [pl.BlockSpec((tm, K), lambda j, i: (i, 0)),
                pl.BlockSpec((K, tn), lambda j, i: (0, j + base // tn))]
    operands = [a, w]
    if shift:
        in_specs.append(pl.BlockSpec((K, LANES), lambda j, i: (0, (j + 1) * (tn // LANES) + base // LANES)))
        operands.append(w)
    out_specs = [pl.BlockSpec((tm, tn), lambda j, i: (i, j))]
    out_shape = [jax.ShapeDtypeStruct((M, n_out), out_dtype)]
    for c in casts:
        nblk = _cast_blocks(c.shape[0], nj * ni)
        spec = pl.BlockSpec((c.shape[0] // nblk, c.shape[1]),
                            lambda j, i, nblk=nblk: (jnp.minimum(j * ni + i, nblk - 1), 0))
        in_specs.append(spec)
        out_specs.append(spec)
        operands.append(c)
        out_shape.append(jax.ShapeDtypeStruct(c.shape, BF16))
    res = pl.pallas_call(
        functools.partial(_ws_kernel, shift=shift if shift else None, n_cast=len(casts)),
        grid=(nj, ni),
        in_specs=in_specs,
        out_specs=out_specs,
        out_shape=out_shape,
        scratch_shapes=[pltpu.VMEM((K, tn), BF16)],
        compiler_params=_cparams(("arbitrary", "arbitrary")),
        name=name,
    )(*operands)
    return res[0], tuple(res[1:])


def _q_up_kernel(cq_ref, g_ref, w_ref, cos_ref, sin_ref, qt_ref, *, heads, scale):
    cn = _rms(cq_ref[...], g_ref[...]).astype(BF16)
    y = jnp.dot(cn, w_ref[...], preferred_element_type=F32) * scale
    c = cos_ref[...]
    s = sin_ref[...]
    half = MLA_ROPE // 2
    for h in range(heads):
        qt_ref[h, 0:MLA_NOPE, :] = y[:, h * MLA_NOPE:(h + 1) * MLA_NOPE].T.astype(BF16)
    rope0 = heads * MLA_NOPE
    for p in range(heads // 2):
        blk = y[:, rope0 + p * LANES: rope0 + (p + 1) * LANES].T
        for u in range(2):
            x1 = blk[u * MLA_ROPE: u * MLA_ROPE + half]
            x2 = blk[u * MLA_ROPE + half: (u + 1) * MLA_ROPE]
            h = 2 * p + u
            qt_ref[h, MLA_NOPE:MLA_NOPE + half, :] = (x1 * c - x2 * s).astype(BF16)
            qt_ref[h, MLA_NOPE + half:MLA_NOPE + MLA_ROPE, :] = (x1 * s + x2 * c).astype(BF16)
    pad = qt_ref.shape[1] - MLA_NOPE - MLA_ROPE
    for h in range(heads):
        qt_ref[h, MLA_NOPE + MLA_ROPE:, :] = jnp.zeros((pad, qt_ref.shape[2]), BF16)


def _q_up(proj_b, cq_blk, g, w, cos_t, sin_t, *, heads, tm=512):
    M = proj_b.shape[0]
    R = g.shape[1]
    tm = _tile(M, tm)
    kq = 2 * LANES
    return pl.pallas_call(
        functools.partial(_q_up_kernel, heads=heads, scale=float((MLA_NOPE + MLA_ROPE) ** -0.5 * math.log2(math.e))),
        grid=(M // tm,),
        in_specs=[
            pl.BlockSpec((tm, R), lambda i: (i, cq_blk)),
            pl.BlockSpec((1, R), lambda i: (0, 0)),
            pl.BlockSpec(w.shape, lambda i: (0, 0)),
            pl.BlockSpec((MLA_ROPE // 2, tm), lambda i: (0, i)),
            pl.BlockSpec((MLA_ROPE // 2, tm), lambda i: (0, i)),
        ],
        out_specs=pl.BlockSpec((heads, kq, tm), lambda i: (0, 0, i)),
        out_shape=jax.ShapeDtypeStruct((heads, kq, M), BF16),
        compiler_params=_cparams(("arbitrary",)),
        name="mla_q_up",
    )(proj_b, g, w, cos_t, sin_t)


def _kv_up_kernel(ckv_ref, kr_ref, g_ref, w_ref, cos_ref, sin_ref, k_ref, vt_ref, *, heads):
    cn = _rms(ckv_ref[...], g_ref[...]).astype(BF16)
    y = jnp.dot(cn, w_ref[...], preferred_element_type=F32)
    x1 = kr_ref[:, 0:LANES]
    x2 = kr_ref[:, LANES:2 * LANES]
    c = cos_ref[...]
    s = sin_ref[...]
    kpe = (x1 * c - x2 * s) + pltpu.roll(x1 * s + x2 * c, MLA_ROPE // 2, axis=1)
    kpe = kpe.astype(BF16)
    hw = MLA_NOPE + MLA_V
    for h in range(heads):
        k_ref[h, :, 0:MLA_NOPE] = y[:, h * hw: h * hw + MLA_NOPE].astype(BF16)
        k_ref[h, :, MLA_NOPE:] = kpe
        vt_ref[h, 0] = y[:, h * hw + MLA_NOPE:(h + 1) * hw].T.astype(BF16)


def _kv_up(proj_b, g, w, cos_p, sin_p, *, heads, ckv_blk, kr_blk, tm):
    M = proj_b.shape[0]
    R = g.shape[1]
    return pl.pallas_call(
        functools.partial(_kv_up_kernel, heads=heads),
        grid=(M // tm,),
        in_specs=[
            pl.BlockSpec((tm, R), lambda i: (i, ckv_blk)),
            pl.BlockSpec((tm, 2 * LANES), lambda i: (i, kr_blk)),
            pl.BlockSpec((1, R), lambda i: (0, 0)),
            pl.BlockSpec(w.shape, lambda i: (0, 0)),
            pl.BlockSpec((tm, LANES), lambda i: (i, 0)),
            pl.BlockSpec((tm, LANES), lambda i: (i, 0)),
        ],
        out_specs=[
            pl.BlockSpec((heads, tm, 2 * LANES), lambda i: (0, i, 0)),
            pl.BlockSpec((heads, 1, MLA_V, tm), lambda i: (0, i, 0, 0)),
        ],
        out_shape=[
            jax.ShapeDtypeStruct((heads, M, 2 * LANES), BF16),
            jax.ShapeDtypeStruct((heads, M // tm, MLA_V, tm), BF16),
        ],
        compiler_params=_cparams(("arbitrary",)),
        name="mla_kv_up",
    )(proj_b, proj_b, g, w, cos_p, sin_p)


def _flash_kernel(qt_ref, k_ref, vt_ref, km_ref, vtm_ref, o_ref, s_ref, p_ref, acc_ref, *, nkb, tk, n_meta, streams):
    ts = qt_ref.shape[1] // streams
    strip = min(tk, 64)
    cols = [slice(c * ts, (c + 1) * ts) for c in range(streams)]

    def scores(c, j, slot):
        s_ref[c, slot] = jnp.dot(k_ref[j * tk:(j + 1) * tk, :], qt_ref[:, cols[c]], preferred_element_type=F32)

    def softmax(c, slot, m, l):
        m_new = jnp.maximum(m, jnp.max(s_ref[c, slot], axis=0, keepdims=True))
        psum = jnp.zeros((8, ts), F32)
        for r in range(0, tk, strip):
            p = jnp.exp2(s_ref[c, slot, r:r + strip, :] - m_new)
            psum = psum + jnp.sum(p.reshape(strip // 8, 8, ts), axis=0)
            p_ref[c, slot, r:r + strip, :] = p.astype(BF16)
        alpha = jnp.exp2(m - m_new)
        return m_new, alpha * l + jnp.sum(psum, axis=0, keepdims=True), alpha

    def values(c, j, slot, alpha):
        acc_ref[c] = alpha * acc_ref[c] + jnp.dot(vt_ref[j], p_ref[c, slot], preferred_element_type=F32)

    stats = []
    for c in range(streams):
        scores(c, 0, 0)
        s_m = jnp.dot(km_ref[...], qt_ref[:, cols[c]], preferred_element_type=F32)
        rows = lax.broadcasted_iota(jnp.int32, s_m.shape, 0)
        s_m = jnp.where(rows < n_meta, s_m, NEG_BIG)
        m = jnp.max(s_m, axis=0, keepdims=True)
        p_m = jnp.exp2(s_m - m)
        l = jnp.sum(p_m, axis=0, keepdims=True)
        acc_ref[c] = jnp.dot(vtm_ref[0], p_m.astype(BF16), preferred_element_type=F32)
        stats.append((m, l, None))
    for j in range(nkb):
        for c in range(streams):
            m, l, alpha = stats[c]
            if j + 1 < nkb:
                scores(c, j + 1, (j + 1) % 2)
            if j > 0:
                values(c, j - 1, (j - 1) % 2, alpha)
            stats[c] = softmax(c, j % 2, m, l)
    for c in range(streams):
        m, l, alpha = stats[c]
        values(c, nkb - 1, (nkb - 1) % 2, alpha)
        o_ref[cols[c], :] = (acc_ref[c] / l).T.astype(o_ref.dtype)


def _flash(qt, k, vt, km, vtm, *, n_meta, tq=1024, streams=4):
    heads, kq, M = qt.shape
    nkb, tk = vt.shape[1], vt.shape[3]
    tq = _tile(M, tq)
    tkm = km.shape[1]
    ts = tq // streams
    return pl.pallas_call(
        functools.partial(_flash_kernel, nkb=nkb, tk=tk, n_meta=n_meta, streams=streams),
        grid=(heads, M // tq),
        in_specs=[
            pl.BlockSpec((None, kq, tq), lambda h, i: (h, 0, i)),
            pl.BlockSpec((None, M, kq), lambda h, i: (h, 0, 0)),
            pl.BlockSpec((None, nkb, MLA_V, tk), lambda h, i: (h, 0, 0, 0)),
            pl.BlockSpec((None, tkm, kq), lambda h, i: (h, 0, 0)),
            pl.BlockSpec((None, 1, MLA_V, tkm), lambda h, i: (h, 0, 0, 0)),
        ],
        out_specs=pl.BlockSpec((tq, MLA_V), lambda h, i: (i, h)),
        out_shape=jax.ShapeDtypeStruct((M, heads * MLA_V), BF16),
        scratch_shapes=[pltpu.VMEM((streams, 2, tk, ts), F32), pltpu.VMEM((streams, 2, tk, ts), BF16),
                        pltpu.VMEM((streams, MLA_V, ts), F32)],
        compiler_params=_cparams(("arbitrary", "arbitrary")),
        name="mla_flash",
    )(qt, k, vt, km, vtm)


def _gla_consts(direction):
    C = GLA_CHUNK
    r = np.arange(C)
    if direction == "fwd":
        cum = (r[None, :] <= r[:, None])
    else:
        cum = (r[None, :] >= r[:, None])
    eq2 = np.zeros((C, C), np.float32)
    ek2 = np.zeros((C, C), np.float32)
    for t in range(C):
        pos = t % 4
        if direction == "fwd":
            if pos == 2:
                eq2[t, t] = 1
            elif pos == 3:
                eq2[t, t] = 1
                eq2[t, t - 1] = 1
            elif pos == 0:
                ek2[t, t + 1] = 1
        else:
            if pos == 1:
                eq2[t, t] = 1
            elif pos == 0:
                eq2[t, t] = 1
                eq2[t, t + 1] = 1
            elif pos == 3:
                ek2[t, t - 1] = 1
    mats = np.concatenate([cum.astype(np.float32), eq2, ek2], axis=0)
    masks = np.zeros((6, C, LANES), np.float32)
    for li, m in enumerate((4, 2, 1)):
        upper = (r % (2 * m)) >= m
        qrows = upper if direction == "fwd" else ~upper
        masks[2 * li] = np.where(qrows, 0.0, NEG_BIG)[:, None]
        masks[2 * li + 1] = np.where(~qrows, 0.0, NEG_BIG)[:, None]
    return jnp.asarray(mats, BF16), jnp.asarray(masks, F32)


def _split3_dot(mat, x):
    x1 = x.astype(BF16)
    r1 = x - x1.astype(F32)
    x2 = r1.astype(BF16)
    x3 = (r1 - x2.astype(F32)).astype(BF16)
    return (jnp.dot(mat, x1, preferred_element_type=F32)
            + jnp.dot(mat, x2, preferred_element_type=F32)
            + jnp.dot(mat, x3, preferred_element_type=F32))


def _log_decay(lr, w_ref, b_ref):
    z = jnp.dot(lr.astype(BF16), w_ref[...], preferred_element_type=F32) + b_ref[...]
    return jax.nn.log_sigmoid(z) * (1.0 / GLA_GATE_NORM)


def _gla_chunk(q, k, v, g, st_ref, c_ref, mats_ref, masks_ref, *, heads, direction, want_out):
    C = GLA_CHUNK
    fwd = direction == "fwd"
    e_all = _split3_dot(mats_ref[...], g)
    c_all = e_all[0:C]
    eq2_all = e_all[C:2 * C]
    ek2_all = e_all[2 * C:3 * C]
    c_ref[...] = c_all
    far = C - 1 if fwd else 0
    outs = []
    for h in range(heads):
        ls = slice(h * GLA_DK, (h + 1) * GLA_DK)
        vs = slice(h * GLA_DV, (h + 1) * GLA_DV)
        kh, gh, ch = k[:, ls], g[:, ls], c_all[:, ls]
        qh = q[:, ls] if want_out else None
        vh = v[:, vs]
        tot = c_ref[pl.ds(far, 1), ls]
        st = st_ref[h]
        if want_out:
            xor = lax.broadcasted_iota(jnp.int32, (C, C), 0) ^ lax.broadcasted_iota(jnp.int32, (C, C), 1)
            dg = jnp.sum(qh * kh, axis=-1, keepdims=True)
            att = jnp.where(xor == 0, dg, 0.0)
            qcat, kcat = [], []
            for m in GLA_LEVELS:
                if m >= 8:
                    qp, kp = [], []
                    for r0 in range(0, C, 2 * m):
                        lo = slice(r0, r0 + m)
                        hi = slice(r0 + m, r0 + 2 * m)
                        ref = c_ref[pl.ds(r0 + m - 1 if fwd else r0 + m, 1), ls]
                        qr, kr = (hi, lo) if fwd else (lo, hi)
                        qa = qh[qr] * jnp.exp(ch[qr] - ref)
                        ka = kh[kr] * jnp.exp(ref - ch[kr])
                        z = jnp.zeros((m, GLA_DK), F32)
                        qp += [z, qa] if fwd else [qa, z]
                        kp += [ka, z] if fwd else [z, ka]
                    qcat.append(jnp.concatenate(qp, axis=0).astype(BF16))
                    kcat.append(jnp.concatenate(kp, axis=0).astype(BF16))
                else:
                    li = (4, 2, 1).index(m)
                    mq = masks_ref[2 * li]
                    mk = masks_ref[2 * li + 1]
                    if m == 4:
                        ref = jnp.concatenate(
                            [jnp.broadcast_to(c_ref[pl.ds(r0 + 3 if fwd else r0 + 4, 1), ls], (8, GLA_DK))
                             for r0 in range(0, C, 8)], axis=0)
                        eq = ch - ref
                        ek = ref - ch
                    elif m == 2:
                        eq = eq2_all[:, ls]
                        ek = ek2_all[:, ls]
                    else:
                        eq = gh
                        ek = jnp.zeros_like(gh)
                    qcat.append((qh * jnp.exp(eq + mq)).astype(BF16))
                    kcat.append((kh * jnp.exp(ek + mk)).astype(BF16))
                pm = lax.dot_general(qcat.pop(), kcat.pop(), (((1,), (1,)), ((), ())),
                                     preferred_element_type=F32)
                att = att + (pm if 2 * m == C else jnp.where(xor < 2 * m, pm, 0.0))
            qs = (qh * jnp.exp(ch)).astype(BF16)
            o = jnp.dot(att.astype(BF16), vh, preferred_element_type=F32)
            o = o + lax.dot_general(qs, st.astype(BF16), (((1,), (1,)), ((), ())),
                                    preferred_element_type=F32)
            outs.append(o)
        kd = (kh * jnp.exp(tot - ch)).astype(BF16)
        ut = lax.dot_general(vh, kd, (((0,), (0,)), ((), ())), preferred_element_type=F32)
        st_ref[h] = st * jnp.exp(tot) + ut
    return outs if want_out else None


def _gla_fwd_kernel(q_ref, k_ref, v_ref, lr_ref, km_ref, vm_ref, lrm_ref, w_ref, b_ref, mats_ref, masks_ref,
                    o_ref, st_ref, c_ref, *, heads, n_meta):
    n = pl.program_id(0)
    common = dict(heads=heads, direction="fwd")

    @pl.when(n == 0)
    def _():
        st_ref[...] = jnp.zeros_like(st_ref)
        g = _log_decay(lrm_ref[...], w_ref, b_ref)
        rows = lax.broadcasted_iota(jnp.int32, g.shape, 0)
        g = jnp.where(rows < n_meta, g, 0.0)
        k = km_ref[...].astype(F32)
        _gla_chunk(None, k, vm_ref[...], g, st_ref, c_ref, mats_ref, masks_ref, want_out=False, **common)

    @pl.when(n > 0)
    def _():
        g = _log_decay(lr_ref[...], w_ref, b_ref)
        q = q_ref[...].astype(F32) * (GLA_DK ** -0.5)
        k = k_ref[...].astype(F32)
        outs = _gla_chunk(q, k, v_ref[...], g, st_ref, c_ref, mats_ref, masks_ref, want_out=True, **common)
        for h, o in enumerate(outs):
            o_ref[:, h * GLA_DV:(h + 1) * GLA_DV] = o


def _gla_bwd_kernel(q_ref, k_ref, v_ref, gog_ref, lr_ref, of_ref, w_ref, b_ref, gn_ref, mats_ref, masks_ref,
                    o_ref, st_ref, c_ref, *, heads):
    @pl.when(pl.program_id(0) == 0)
    def _():
        st_ref[...] = jnp.zeros_like(st_ref)

    g = _log_decay(lr_ref[...], w_ref, b_ref)
    q = q_ref[...].astype(F32) * (GLA_DK ** -0.5)
    k = k_ref[...].astype(F32)
    outs = _gla_chunk(q, k, v_ref[...], g, st_ref, c_ref, mats_ref, masks_ref,
                      heads=heads, direction="bwd", want_out=True)
    gn = gn_ref[...]
    for h, ob in enumerate(outs):
        vs = slice(h * GLA_DV, (h + 1) * GLA_DV)
        o = _rms(of_ref[:, vs] + ob, gn)
        o_ref[:, vs] = (o * jax.nn.silu(gog_ref[:, vs].astype(F32))).astype(o_ref.dtype)


def _gla(proj_ac, proj_b, km, vm, lrm, wf, bf, wb, bb, gn, *, heads, n_meta):
    M = proj_ac.shape[0]
    C = GLA_CHUNK
    nch = M // C
    qw, vw = heads * GLA_DK, heads * GLA_DV
    const2 = lambda n: (0, 0)
    const3 = lambda n: (0, 0, 0)
    scratch = [pltpu.VMEM((heads, GLA_DV, GLA_DK), F32), pltpu.VMEM((C, qw), F32)]
    mats_f, masks_f = _gla_consts("fwd")
    mats_b, masks_b = _gla_consts("bwd")

    xb = lambda n: jnp.maximum(n - 1, 0)
    o_f = pl.pallas_call(
        functools.partial(_gla_fwd_kernel, heads=heads, n_meta=n_meta),
        grid=(nch + 1,),
        in_specs=[
            pl.BlockSpec((C, qw), lambda n: (xb(n), 0)),
            pl.BlockSpec((C, qw), lambda n: (xb(n), 1)),
            pl.BlockSpec((C, vw), lambda n: (xb(n), 1)),
            pl.BlockSpec((C, LANES), lambda n: (xb(n), 0)),
            pl.BlockSpec(km.shape, const2),
            pl.BlockSpec(vm.shape, const2),
            pl.BlockSpec((C, LANES), const2),
            pl.BlockSpec(wf.shape, const2),
            pl.BlockSpec(bf.shape, const2),
            pl.BlockSpec(mats_f.shape, const2),
            pl.BlockSpec(masks_f.shape, const3),
        ],
        out_specs=pl.BlockSpec((C, vw), lambda n: (xb(n), 0)),
        out_shape=jax.ShapeDtypeStruct((M, vw), F32),
        scratch_shapes=scratch,
        compiler_params=_cparams(("arbitrary",)),
        name="gla_fwd",
    )(proj_ac, proj_ac, proj_ac, proj_b, km, vm, lrm, wf, bf, mats_f, masks_f)

    rb = lambda n: nch - 1 - n
    return pl.pallas_call(
        functools.partial(_gla_bwd_kernel, heads=heads),
        grid=(nch,),
        in_specs=[
            pl.BlockSpec((C, qw), lambda n: (rb(n), 0)),
            pl.BlockSpec((C, qw), lambda n: (rb(n), 1)),
            pl.BlockSpec((C, vw), lambda n: (rb(n), 1)),
            pl.BlockSpec((C, vw), lambda n: (rb(n), 2)),
            pl.BlockSpec((C, LANES), lambda n: (rb(n), 1)),
            pl.BlockSpec((C, vw), lambda n: (rb(n), 0)),
            pl.BlockSpec(wb.shape, const2),
            pl.BlockSpec(bb.shape, const2),
            pl.BlockSpec(gn.shape, const2),
            pl.BlockSpec(mats_b.shape, const2),
            pl.BlockSpec(masks_b.shape, const3),
        ],
        out_specs=pl.BlockSpec((C, vw), lambda n: (rb(n), 0)),
        out_shape=jax.ShapeDtypeStruct((M, vw), BF16),
        scratch_shapes=scratch,
        compiler_params=_cparams(("arbitrary",)),
        name="gla_bwd",
    )(proj_ac, proj_ac, proj_ac, proj_ac, proj_b, o_f, wb, bb, gn, mats_b, masks_b)


def _merge_kernel(og_ref, om_ref, wg_ref, wm_ref, ga_ref, gb_ref, o_ref):
    yg = jnp.dot(og_ref[...], wg_ref[...], preferred_element_type=F32)
    ym = jnp.dot(om_ref[...], wm_ref[...], preferred_element_type=F32)
    ga = jax.nn.sigmoid(ga_ref[...].astype(F32))
    gb = jax.nn.sigmoid(gb_ref[...].astype(F32))
    o_ref[...] = (ga * yg + gb * ym).astype(o_ref.dtype)


def _merge(og, om, wg, wm, gates, *, tm=1024, tn=1024):
    M, D = og.shape[0], wg.shape[1]
    tm, tn = _tile(M, tm), _tile(D, tn)
    gb_blk = D // tn
    return pl.pallas_call(
        _merge_kernel,
        grid=(M // tm, D // tn),
        in_specs=[
            pl.BlockSpec((tm, og.shape[1]), lambda i, j: (i, 0)),
            pl.BlockSpec((tm, om.shape[1]), lambda i, j: (i, 0)),
            pl.BlockSpec((wg.shape[0], tn), lambda i, j: (0, j)),
            pl.BlockSpec((wm.shape[0], tn), lambda i, j: (0, j)),
            pl.BlockSpec((tm, tn), lambda i, j: (i, j)),
            pl.BlockSpec((tm, tn), lambda i, j: (i, j + gb_blk)),
        ],
        out_specs=pl.BlockSpec((tm, tn), lambda i, j: (i, j)),
        out_shape=jax.ShapeDtypeStruct((M, D), BF16),
        compiler_params=_cparams(("arbitrary", "arbitrary")),
        name="merge_out_proj",
    )(og, om, wg, wm, gates, gates)


def _rmm_kernel(a_ref, b_ref, r_ref, o_ref, *acc, nk):
    part = jnp.dot(a_ref[...], b_ref[...], preferred_element_type=F32)
    if nk == 1:
        o_ref[...] = r_ref[...] + part
        return
    acc_ref, = acc
    kk = pl.program_id(2)

    @pl.when(kk == 0)
    def _():
        acc_ref[...] = part

    @pl.when(kk > 0)
    def _():
        acc_ref[...] += part

    @pl.when(kk == nk - 1)
    def _():
        o_ref[...] = r_ref[...] + acc_ref[...]


def _res_matmul(a, b, res, *, tm=1024, tn=1024, tk=4096, name):
    M, K = a.shape
    N = b.shape[1]
    tm, tn, tk = _tile(M, tm), _tile(N, tn), _tile(K, tk)
    nk = K // tk
    return pl.pallas_call(
        functools.partial(_rmm_kernel, nk=nk),
        grid=(M // tm, N // tn, nk),
        in_specs=[
            pl.BlockSpec((tm, tk), lambda i, j, k: (i, k)),
            pl.BlockSpec((tk, tn), lambda i, j, k: (k, j)),
            pl.BlockSpec((tm, tn), lambda i, j, k: (i, j)),
        ],
        out_specs=pl.BlockSpec((tm, tn), lambda i, j, k: (i, j)),
        out_shape=jax.ShapeDtypeStruct((M, N), F32),
        scratch_shapes=[pltpu.VMEM((tm, tn), F32)] if nk > 1 else [],
        compiler_params=_cparams(("arbitrary", "arbitrary", "arbitrary")),
        name=name,
    )(a, b, res)


def _pad_cols(w, width):
    return jnp.pad(w, ((0, 0), (0, width - w.shape[1])))


def kernel(x, meta_tokens, ln1, w_in, gla_wf, gla_bf, gla_wb, gla_bb, gla_norm, q_norm, w_uq, kv_norm, w_ukv,
           w_gla_out, w_mla_out, w_o, ln2, w_ff1, w_ff2, final_norm):
    assert x.shape[0] == 1 and ln1.shape[0] == 1, "one sequence, one layer"
    S, D = x.shape[1], x.shape[2]
    n_meta = meta_tokens.shape[0]
    rank = gla_wf.shape[1]
    qk_w = gla_wf.shape[2]
    hg = qk_w // GLA_DK
    v_w = hg * GLA_DV
    q_rank, kv_rank = w_uq.shape[1], w_ukv.shape[1]
    hm = w_uq.shape[2] // (MLA_NOPE + MLA_ROPE)
    half = MLA_ROPE // 2
    assert n_meta <= GLA_CHUNK and rank <= LANES and hm % 2 == 0 and S % GLA_CHUNK == 0

    w = w_in[0]
    o_gog_end = 2 * qk_w + 2 * v_w
    o_lrb = o_gog_end + rank
    o_cq = o_lrb + rank
    o_ckv = o_cq + q_rank
    o_kr = o_ckv + kv_rank
    o_ga = o_kr + MLA_ROPE
    w_b = jnp.concatenate([
        _pad_cols(w[:, o_gog_end:o_lrb], LANES), _pad_cols(w[:, o_lrb:o_cq], LANES),
        _pad_cols(w[:, o_kr:o_kr + half], LANES), _pad_cols(w[:, o_kr + half:o_ga], LANES),
        w[:, o_ckv:o_kr], w[:, o_cq:o_ckv]], axis=1)
    kr_blk = 1
    ckv_blk = (4 * LANES) // kv_rank
    cq_blk = (4 * LANES + kv_rank) // q_rank
    assert (4 * LANES) % kv_rank == 0 and (4 * LANES + kv_rank) % q_rank == 0

    wq = w_uq[0].reshape(q_rank, hm, MLA_NOPE + MLA_ROPE)
    wq = jnp.concatenate([wq[:, :, :MLA_NOPE].reshape(q_rank, -1), wq[:, :, MLA_NOPE:].reshape(q_rank, -1)],
                         axis=1).astype(BF16)
    wkv = w_ukv[0].astype(BF16)
    wf = jnp.pad(gla_wf[0], ((0, LANES - rank), (0, 0))).astype(BF16)
    wb = jnp.pad(gla_wb[0], ((0, LANES - rank), (0, 0))).astype(BF16)

    inv_freq = ROPE_THETA ** (-jnp.arange(0, MLA_ROPE, 2, dtype=F32) / MLA_ROPE)
    ang = jnp.arange(n_meta + S, dtype=F32)[:, None] * inv_freq[None, :]
    cos, sin = jnp.cos(ang), jnp.sin(ang)
    cos_x, sin_x = cos[n_meta:], sin[n_meta:]
    cos_xp, sin_xp = _pad_cols(cos_x, LANES), _pad_cols(sin_x, LANES)
    mrows = GLA_CHUNK
    padm = lambda a: jnp.pad(a, ((0, mrows - n_meta), (0, LANES - a.shape[1])))
    cos_mp, sin_mp = padm(cos[:n_meta]), padm(sin[:n_meta])

    xs = x[0]
    xm = jnp.pad(meta_tokens.astype(F32), ((0, mrows - n_meta), (0, 0)))
    g1 = ln1[0][None, :]

    xn = _rms_rows(xs, g1, out_dtype=BF16, name="ln1_x")
    xmn = _rms_rows(xm, g1, out_dtype=BF16, name="ln1_meta")
    proj_a, (wo, wgo, wmo) = _ws_matmul(xn, w, col0=0, n_out=o_gog_end, out_dtype=BF16, tm=1024, tn=512,
                                        casts=(w_o[0], w_gla_out[0], w_mla_out[0]), name="in_proj_a")
    gates, (w1,) = _ws_matmul(xn, w, col0=o_ga, n_out=2 * D, out_dtype=BF16, tm=1024, tn=512,
                              casts=(w_ff1[0],), name="in_proj_gates")
    proj_b, _ = _ws_matmul(xn, w_b, col0=0, n_out=w_b.shape[1], out_dtype=F32, tm=1024, tn=512, name="in_proj_b")
    proj_b_m, _ = _ws_matmul(xmn, w_b, col0=0, n_out=w_b.shape[1], out_dtype=F32, tm=mrows, tn=512,
                             name="in_proj_b_meta")
    kv_m, _ = _ws_matmul(xmn, w, col0=qk_w, n_out=qk_w + v_w, out_dtype=BF16, tm=mrows, tn=512,
                         name="in_proj_kv_meta")

    og = _gla(proj_a, proj_b, kv_m[:, :qk_w], kv_m[:, qk_w:], proj_b_m[:, :LANES],
              wf, gla_bf, wb, gla_bb, gla_norm, heads=hg, n_meta=n_meta)

    qt = _q_up(proj_b, cq_blk, q_norm, wq, cos_x.T, sin_x.T, heads=hm)
    gkv = kv_norm
    k_x, vt_x = _kv_up(proj_b, gkv, wkv, cos_xp, sin_xp, heads=hm, ckv_blk=ckv_blk, kr_blk=kr_blk,
                       tm=_tile(S, 512))
    k_m, vt_m = _kv_up(proj_b_m, gkv, wkv, cos_mp, sin_mp, heads=hm, ckv_blk=ckv_blk, kr_blk=kr_blk, tm=mrows)
    om = _flash(qt, k_x, vt_x, k_m, vt_m, n_meta=n_meta)

    merged = _merge(og, om, wgo, wmo, gates)
    h1 = _res_matmul(merged, wo, xs, name="o_proj")
    hf, w2 = _norm_matmul(h1, ln2[0][None, :], w1, w_ff2[0], out_dtype=BF16, act="relu2", name="ffn_up")
    h2 = _res_matmul(hf, w2, h1, tk=2048, name="ffn_down")
    return _rms_rows(h2, final_norm[None, :], out_dtype=F32, name="final_norm")[None]
```

```python
import functools
import math

import jax
import jax.numpy as jnp
import numpy as np
from jax import lax
from jax.experimental import pallas as pl
from jax.experimental.pallas import tpu as pltpu

F32 = jnp.float32
BF16 = jnp.bfloat16

GLA_DK = 128
GLA_DV = 256
GLA_GATE_NORM = 16.0
MLA_NOPE = 128
MLA_ROPE = 64
MLA_V = 128
ROPE_THETA = 10000.0
EPS = 1e-6

LANES = 128
GLA_CHUNK = 128
GLA_LEVELS = (64, 32, 16, 8, 4, 2, 1)
NEG_BIG = -1e30
VMEM_LIMIT = 56 * 1024 * 1024


def _cparams(sem):
    return pltpu.CompilerParams(dimension_semantics=sem, vmem_limit_bytes=VMEM_LIMIT)


def _tile(n, pref):
    t = min(n, pref)
    while n % t:
        t //= 2
    return t


def _rms(a, g, eps=EPS):
    ms = jnp.mean(a * a, axis=-1, keepdims=True)
    return a * lax.rsqrt(ms + eps) * g


def _nmm_kernel(a_ref, g_ref, b_ref, o_ref, an_ref, *, act):
    @pl.when(pl.program_id(1) == 0)
    def _():
        an_ref[...] = _rms(a_ref[...], g_ref[...]).astype(BF16)

    y = jnp.dot(an_ref[...], b_ref[...], preferred_element_type=F32)
    if act == "relu2":
        y = jnp.square(jnp.maximum(y, 0.0))
    o_ref[...] = y.astype(o_ref.dtype)


def _norm_matmul(a, g, b, *, out_dtype, tm=512, tn=1024, act=None, name):
    M, K = a.shape
    N = b.shape[1]
    tm, tn = _tile(M, tm), _tile(N, tn)
    return pl.pallas_call(
        functools.partial(_nmm_kernel, act=act),
        grid=(M // tm, N // tn),
        in_specs=[
            pl.BlockSpec((tm, K), lambda i, j: (i, 0)),
            pl.BlockSpec((1, K), lambda i, j: (0, 0)),
            pl.BlockSpec((K, tn), lambda i, j: (0, j)),
        ],
        out_specs=pl.BlockSpec((tm, tn), lambda i, j: (i, j)),
        out_shape=jax.ShapeDtypeStruct((M, N), out_dtype),
        scratch_shapes=[pltpu.VMEM((tm, K), BF16)],
        compiler_params=_cparams(("arbitrary", "arbitrary")),
        name=name,
    )(a, g, b)


def _rms_rows_kernel(x_ref, g_ref, o_ref):
    o_ref[...] = _rms(x_ref[...], g_ref[...]).astype(o_ref.dtype)


def _rms_rows(x, g, *, out_dtype, tm=256, name):
    M, D = x.shape
    tm = _tile(M, tm)
    return pl.pallas_call(
        _rms_rows_kernel,
        grid=(M // tm,),
        in_specs=[pl.BlockSpec((tm, D), lambda i: (i, 0)), pl.BlockSpec((1, D), lambda i: (0, 0))],
        out_specs=pl.BlockSpec((tm, D), lambda i: (i, 0)),
        out_shape=jax.ShapeDtypeStruct((M, D), out_dtype),
        compiler_params=_cparams(("arbitrary",)),
        name=name,
    )(x, g)


def _ws_kernel(*refs, shift):
    a_ref, w_ref = refs[0], refs[1]
    w2_ref = refs[2] if shift is not None else None
    o_ref, wb_ref = refs[-2], refs[-1]
    K, tn = wb_ref.shape

    @pl.when(pl.program_id(1) == 0)
    def _():
        kc = _tile(K, 512)
        for r in range(0, K, kc):
            if shift is None:
                blk = w_ref[:, r:r + kc]
            else:
                blk = jnp.concatenate([w_ref[shift:, r:r + kc], w2_ref[:shift, r:r + kc]], axis=0)
            wb_ref[r:r + kc, :] = blk.T.astype(BF16)

    o_ref[...] = jnp.dot(a_ref[...], wb_ref[...], preferred_element_type=F32).astype(o_ref.dtype)


def _ws_matmul(a, wt, *, col0, n_out, out_dtype, tm, tn, name):
    M, K = a.shape
    shift = col0 % LANES
    base = col0 - shift
    tm, tn = _tile(M, tm), _tile(math.gcd(n_out, base), tn)
    nj, ni = n_out // tn, M // tm
    in_specs = [pl.BlockSpec((tm, K), lambda j, i: (i, 0)),
                pl.BlockSpec((tn, K), lambda j, i: (j + base // tn, 0))]
    operands = [a, wt]
    if shift:
        in_specs.append(pl.BlockSpec((LANES, K), lambda j, i: ((j + 1) * (tn // LANES) + base // LANES, 0)))
        operands.append(wt)
    return pl.pallas_call(
        functools.partial(_ws_kernel, shift=shift if shift else None),
        grid=(nj, ni),
        in_specs=in_specs,
        out_specs=pl.BlockSpec((tm, tn), lambda j, i: (i, j)),
        out_shape=jax.ShapeDtypeStruct((M, n_out), out_dtype),
        scratch_shapes=[pltpu.VMEM((K, tn), BF16)],
        compiler_params=_cparams(("arbitrary", "arbitrary")),
        name=name,
    )(*operands)


def _q_up_kernel(cq_ref, g_ref, w_ref, cos_ref, sin_ref, qt_ref, *, heads, scale):
    cn = _rms(cq_ref[...], g_ref[...]).astype(BF16)
    y = jnp.dot(cn, w_ref[...], preferred_element_type=F32) * scale
    c = cos_ref[...]
    s = sin_ref[...]
    half = MLA_ROPE // 2
    for h in range(heads):
        qt_ref[h, 0:MLA_NOPE, :] = y[:, h * MLA_NOPE:(h + 1) * MLA_NOPE].T.astype(BF16)
    rope0 = heads * MLA_NOPE
    for p in range(heads // 2):
        blk = y[:, rope0 + p * LANES: rope0 + (p + 1) * LANES].T
        for u in range(2):
            x1 = blk[u * MLA_ROPE: u * MLA_ROPE + half]
            x2 = blk[u * MLA_ROPE + half: (u + 1) * MLA_ROPE]
            h = 2 * p + u
            qt_ref[h, MLA_NOPE:MLA_NOPE + half, :] = (x1 * c - x2 * s).astype(BF16)
            qt_ref[h, MLA_NOPE + half:MLA_NOPE + MLA_ROPE, :] = (x1 * s + x2 * c).astype(BF16)
    pad = qt_ref.shape[1] - MLA_NOPE - MLA_ROPE
    for h in range(heads):
        qt_ref[h, MLA_NOPE + MLA_ROPE:, :] = jnp.zeros((pad, qt_ref.shape[2]), BF16)


def _q_up(proj_b, cq_blk, g, w, cos_t, sin_t, *, heads, tm=512):
    M = proj_b.shape[0]
    R = g.shape[1]
    tm = _tile(M, tm)
    kq = 2 * LANES
    return pl.pallas_call(
        functools.partial(_q_up_kernel, heads=heads, scale=float((MLA_NOPE + MLA_ROPE) ** -0.5 * math.log2(math.e))),
        grid=(M // tm,),
        in_specs=[
            pl.BlockSpec((tm, R), lambda i: (i, cq_blk)),
            pl.BlockSpec((1, R), lambda i: (0, 0)),
            pl.BlockSpec(w.shape, lambda i: (0, 0)),
            pl.BlockSpec((MLA_ROPE // 2, tm), lambda i: (0, i)),
            pl.BlockSpec((MLA_ROPE // 2, tm), lambda i: (0, i)),
        ],
        out_specs=pl.BlockSpec((heads, kq, tm), lambda i: (0, 0, i)),
        out_shape=jax.ShapeDtypeStruct((heads, kq, M), BF16),
        compiler_params=_cparams(("arbitrary",)),
        name="mla_q_up",
    )(proj_b, g, w, cos_t, sin_t)


def _kv_up_kernel(ckv_ref, kr_ref, g_ref, w_ref, cos_ref, sin_ref, k_ref, vt_ref, *, heads):
    cn = _rms(ckv_ref[...], g_ref[...]).astype(BF16)
    y = jnp.dot(cn, w_ref[...], preferred_element_type=F32)
    x1 = kr_ref[:, 0:LANES]
    x2 = kr_ref[:, LANES:2 * LANES]
    c = cos_ref[...]
    s = sin_ref[...]
    kpe = (x1 * c - x2 * s) + pltpu.roll(x1 * s + x2 * c, MLA_ROPE // 2, axis=1)
    kpe = kpe.astype(BF16)
    hw = MLA_NOPE + MLA_V
    for h in range(heads):
        k_ref[h, :, 0:MLA_NOPE] = y[:, h * hw: h * hw + MLA_NOPE].astype(BF16)
        k_ref[h, :, MLA_NOPE:] = kpe
        vt_ref[h, 0] = y[:, h * hw + MLA_NOPE:(h + 1) * hw].T.astype(BF16)


def _kv_up(proj_b, g, w, cos_p, sin_p, *, heads, ckv_blk, kr_blk, tm):
    M = proj_b.shape[0]
    R = g.shape[1]
    return pl.pallas_call(
        functools.partial(_kv_up_kernel, heads=heads),
        grid=(M // tm,),
        in_specs=[
            pl.BlockSpec((tm, R), lambda i: (i, ckv_blk)),
            pl.BlockSpec((tm, 2 * LANES), lambda i: (i, kr_blk)),
            pl.BlockSpec((1, R), lambda i: (0, 0)),
            pl.BlockSpec(w.shape, lambda i: (0, 0)),
            pl.BlockSpec((tm, LANES), lambda i: (i, 0)),
            pl.BlockSpec((tm, LANES), lambda i: (i, 0)),
        ],
        out_specs=[
            pl.BlockSpec((heads, tm, 2 * LANES), lambda i: (0, i, 0)),
            pl.BlockSpec((heads, 1, MLA_V, tm), lambda i: (0, i, 0, 0)),
        ],
        out_shape=[
            jax.ShapeDtypeStruct((heads, M, 2 * LANES), BF16),
            jax.ShapeDtypeStruct((heads, M // tm, MLA_V, tm), BF16),
        ],
        compiler_params=_cparams(("arbitrary",)),
        name="mla_kv_up",
    )(proj_b, proj_b, g, w, cos_p, sin_p)


def _flash_kernel(*refs, nkb, tk, n_meta, streams, n_cast):
    qt_ref, k_ref, vt_ref, km_ref, vtm_ref = refs[:5]
    cast_in = refs[5:5 + n_cast]
    o_ref = refs[5 + n_cast]
    cast_out = refs[6 + n_cast:6 + 2 * n_cast]
    s_ref, p_ref, acc_ref = refs[6 + 2 * n_cast:]
    for ci, co in zip(cast_in, cast_out):
        co[...] = ci[...].astype(BF16)
    ts = qt_ref.shape[1] // streams
    strip = min(tk, 64)
    cols = [slice(c * ts, (c + 1) * ts) for c in range(streams)]

    def scores(c, j, slot):
        s_ref[c, slot] = jnp.dot(k_ref[j * tk:(j + 1) * tk, :], qt_ref[:, cols[c]], preferred_element_type=F32)

    def softmax(c, slot, m, l):
        mx = jnp.full((8, ts), NEG_BIG, F32)
        for r in range(0, tk, strip):
            mx = jnp.maximum(mx, jnp.max(s_ref[c, slot, r:r + strip, :].reshape(strip // 8, 8, ts), axis=0))
        m_new = jnp.maximum(m, jnp.max(mx, axis=0, keepdims=True))
        psum = jnp.zeros((8, ts), F32)
        for r in range(0, tk, strip):
            p = jnp.exp2(s_ref[c, slot, r:r + strip, :] - m_new)
            psum = psum + jnp.sum(p.reshape(strip // 8, 8, ts), axis=0)
            p_ref[c, slot, r:r + strip, :] = p.astype(BF16)
        alpha = jnp.exp2(m - m_new)
        return m_new, alpha * l + jnp.sum(psum, axis=0, keepdims=True), alpha

    def values(c, j, slot, alpha):
        acc_ref[c] = alpha * acc_ref[c] + jnp.dot(vt_ref[j], p_ref[c, slot], preferred_element_type=F32)

    stats = []
    for c in range(streams):
        scores(c, 0, 0)
        s_m = jnp.dot(km_ref[...], qt_ref[:, cols[c]], preferred_element_type=F32)
        rows = lax.broadcasted_iota(jnp.int32, s_m.shape, 0)
        s_m = jnp.where(rows < n_meta, s_m, NEG_BIG)
        m = jnp.max(s_m, axis=0, keepdims=True)
        p_m = jnp.exp2(s_m - m)
        l = jnp.sum(p_m, axis=0, keepdims=True)
        acc_ref[c] = jnp.dot(vtm_ref[0], p_m.astype(BF16), preferred_element_type=F32)
        stats.append((m, l, None))
    for j in range(nkb):
        for c in range(streams):
            m, l, alpha = stats[c]
            if j + 1 < nkb:
                scores(c, j + 1, (j + 1) % 2)
            if j > 0:
                values(c, j - 1, (j - 1) % 2, alpha)
            stats[c] = softmax(c, j % 2, m, l)
    for c in range(streams):
        m, l, alpha = stats[c]
        values(c, nkb - 1, (nkb - 1) % 2, alpha)
        o_ref[cols[c], :] = (acc_ref[c] / l).T.astype(o_ref.dtype)


def _cast_blocks(rows, steps):
    d = max(1, min(rows // 16, steps))
    while (rows // 16) % d:
        d -= 1
    return d


def _flash(qt, k, vt, km, vtm, casts, *, n_meta, tq=1024, streams=4):
    heads, kq, M = qt.shape
    nkb, tk = vt.shape[1], vt.shape[3]
    tq = _tile(M, tq)
    tkm = km.shape[1]
    ts = tq // streams
    nq = M // tq
    cast_specs = []
    for c in casts:
        nblk = _cast_blocks(c.shape[0], heads * nq)
        cast_specs.append(pl.BlockSpec((c.shape[0] // nblk, c.shape[1]),
                                       lambda h, i, nblk=nblk: (jnp.minimum(h * nq + i, nblk - 1), 0)))
    res = pl.pallas_call(
        functools.partial(_flash_kernel, nkb=nkb, tk=tk, n_meta=n_meta, streams=streams, n_cast=len(casts)),
        grid=(heads, nq),
        in_specs=[
            pl.BlockSpec((None, kq, tq), lambda h, i: (h, 0, i)),
            pl.BlockSpec((None, M, kq), lambda h, i: (h, 0, 0)),
            pl.BlockSpec((None, nkb, MLA_V, tk), lambda h, i: (h, 0, 0, 0)),
            pl.BlockSpec((None, tkm, kq), lambda h, i: (h, 0, 0)),
            pl.BlockSpec((None, 1, MLA_V, tkm), lambda h, i: (h, 0, 0, 0)),
        ] + cast_specs,
        out_specs=[pl.BlockSpec((tq, MLA_V), lambda h, i: (i, h))] + cast_specs,
        out_shape=[jax.ShapeDtypeStruct((M, heads * MLA_V), BF16)]
        + [jax.ShapeDtypeStruct(c.shape, BF16) for c in casts],
        scratch_shapes=[pltpu.VMEM((streams, 2, tk, ts), F32), pltpu.VMEM((streams, 2, tk, ts), BF16),
                        pltpu.VMEM((streams, MLA_V, ts), F32)],
        compiler_params=_cparams(("arbitrary", "arbitrary")),
        name="mla_flash",
    )(qt, k, vt, km, vtm, *casts)
    return res[0], tuple(res[1:])


def _gla_consts(direction):
    C = GLA_CHUNK
    r = np.arange(C)
    if direction == "fwd":
        cum = (r[None, :] <= r[:, None])
    else:
        cum = (r[None, :] >= r[:, None])
    eq2 = np.zeros((C, C), np.float32)
    ek2 = np.zeros((C, C), np.float32)
    for t in range(C):
        pos = t % 4
        if direction == "fwd":
            if pos == 2:
                eq2[t, t] = 1
            elif pos == 3:
                eq2[t, t] = 1
                eq2[t, t - 1] = 1
            elif pos == 0:
                ek2[t, t + 1] = 1
        else:
            if pos == 1:
                eq2[t, t] = 1
            elif pos == 0:
                eq2[t, t] = 1
                eq2[t, t + 1] = 1
            elif pos == 3:
                ek2[t, t - 1] = 1
    mats = np.concatenate([cum.astype(np.float32), eq2, ek2], axis=0)
    masks = np.zeros((6, C, LANES), np.float32)
    for li, m in enumerate((4, 2, 1)):
        upper = (r % (2 * m)) >= m
        qrows = upper if direction == "fwd" else ~upper
        masks[2 * li] = np.where(qrows, 0.0, NEG_BIG)[:, None]
        masks[2 * li + 1] = np.where(~qrows, 0.0, NEG_BIG)[:, None]
    return jnp.asarray(mats, BF16), jnp.asarray(masks, F32)


def _split3_dot(mat, x):
    x1 = x.astype(BF16)
    r1 = x - x1.astype(F32)
    x2 = r1.astype(BF16)
    x3 = (r1 - x2.astype(F32)).astype(BF16)
    return (jnp.dot(mat, x1, preferred_element_type=F32)
            + jnp.dot(mat, x2, preferred_element_type=F32)
            + jnp.dot(mat, x3, preferred_element_type=F32))


def _log_decay(lr, w_ref, b_ref):
    z = jnp.dot(lr.astype(BF16), w_ref[...], preferred_element_type=F32) + b_ref[...]
    return jax.nn.log_sigmoid(z) * (math.log2(math.e) / GLA_GATE_NORM)


def _gla_chunk(q, k, v, g, st_ref, c_ref, mats_ref, masks_ref, *, heads, direction, want_out):
    C = GLA_CHUNK
    fwd = direction == "fwd"
    e_all = _split3_dot(mats_ref[...], g)
    c_all = e_all[0:C]
    eq2_all = e_all[C:2 * C]
    ek2_all = e_all[2 * C:3 * C]
    c_ref[...] = c_all
    far = C - 1 if fwd else 0
    outs = []
    for h in range(heads):
        ls = slice(h * GLA_DK, (h + 1) * GLA_DK)
        vs = slice(h * GLA_DV, (h + 1) * GLA_DV)
        kh, gh, ch = k[:, ls], g[:, ls], c_all[:, ls]
        qh = q[:, ls] if want_out else None
        vh = v[:, vs]
        tot = c_ref[pl.ds(far, 1), ls]
        st = st_ref[h]
        if want_out:
            xor = lax.broadcasted_iota(jnp.int32, (C, C), 0) ^ lax.broadcasted_iota(jnp.int32, (C, C), 1)
            dg = jnp.sum(qh * kh, axis=-1, keepdims=True)
            att = jnp.where(xor == 0, dg, 0.0)
            qcat, kcat = [], []
            for m in GLA_LEVELS:
                if m >= 8:
                    qp, kp = [], []
                    for r0 in range(0, C, 2 * m):
                        lo = slice(r0, r0 + m)
                        hi = slice(r0 + m, r0 + 2 * m)
                        ref = c_ref[pl.ds(r0 + m - 1 if fwd else r0 + m, 1), ls]
                        qr, kr = (hi, lo) if fwd else (lo, hi)
                        qa = qh[qr] * jnp.exp2(ch[qr] - ref)
                        ka = kh[kr] * jnp.exp2(ref - ch[kr])
                        z = jnp.zeros((m, GLA_DK), F32)
                        qp += [z, qa] if fwd else [qa, z]
                        kp += [ka, z] if fwd else [z, ka]
                    qcat.append(jnp.concatenate(qp, axis=0).astype(BF16))
                    kcat.append(jnp.concatenate(kp, axis=0).astype(BF16))
                else:
                    li = (4, 2, 1).index(m)
                    mq = masks_ref[2 * li]
                    mk = masks_ref[2 * li + 1]
                    if m == 4:
                        ref = jnp.concatenate(
                            [jnp.broadcast_to(c_ref[pl.ds(r0 + 3 if fwd else r0 + 4, 1), ls], (8, GLA_DK))
                             for r0 in range(0, C, 8)], axis=0)
                        eq = ch - ref
                        ek = ref - ch
                    elif m == 2:
                        eq = eq2_all[:, ls]
                        ek = ek2_all[:, ls]
                    else:
                        eq = gh
                        ek = jnp.zeros_like(gh)
                    qcat.append((qh * jnp.exp2(eq + mq)).astype(BF16))
                    kcat.append((kh * jnp.exp2(ek + mk)).astype(BF16))
                pm = lax.dot_general(qcat.pop(), kcat.pop(), (((1,), (1,)), ((), ())),
                                     preferred_element_type=F32)
                att = att + (pm if 2 * m == C else jnp.where(xor < 2 * m, pm, 0.0))
            qs = (qh * jnp.exp2(ch)).astype(BF16)
            o = jnp.dot(att.astype(BF16), vh, preferred_element_type=F32)
            o = o + lax.dot_general(qs, st.astype(BF16), (((1,), (1,)), ((), ())),
                                    preferred_element_type=F32)
            outs.append(o)
        kd = (kh * jnp.exp2(tot - ch)).astype(BF16)
        ut = lax.dot_general(vh, kd, (((0,), (0,)), ((), ())), preferred_element_type=F32)
        st_ref[h] = st * jnp.exp2(tot) + ut
    return outs if want_out else None


def _gla_fwd_kernel(q_ref, k_ref, v_ref, lr_ref, km_ref, vm_ref, lrm_ref, w_ref, b_ref, mats_ref, masks_ref,
                    o_ref, st_ref, c_ref, *, heads, n_meta):
    n = pl.program_id(0)
    common = dict(heads=heads, direction="fwd")

    @pl.when(n == 0)
    def _():
        st_ref[...] = jnp.zeros_like(st_ref)
        g = _log_decay(lrm_ref[...], w_ref, b_ref)
        rows = lax.broadcasted_iota(jnp.int32, g.shape, 0)
        g = jnp.where(rows < n_meta, g, 0.0)
        k = km_ref[...].astype(F32)
        _gla_chunk(None, k, vm_ref[...], g, st_ref, c_ref, mats_ref, masks_ref, want_out=False, **common)

    @pl.when(n > 0)
    def _():
        g = _log_decay(lr_ref[...], w_ref, b_ref)
        q = q_ref[...].astype(F32) * (GLA_DK ** -0.5)
        k = k_ref[...].astype(F32)
        outs = _gla_chunk(q, k, v_ref[...], g, st_ref, c_ref, mats_ref, masks_ref, want_out=True, **common)
        for h, o in enumerate(outs):
            o_ref[:, h * GLA_DV:(h + 1) * GLA_DV] = o


def _gla_bwd_kernel(q_ref, k_ref, v_ref, gog_ref, lr_ref, of_ref, w_ref, b_ref, gn_ref, mats_ref, masks_ref,
                    o_ref, st_ref, c_ref, *, heads):
    @pl.when(pl.program_id(0) == 0)
    def _():
        st_ref[...] = jnp.zeros_like(st_ref)

    g = _log_decay(lr_ref[...], w_ref, b_ref)
    q = q_ref[...].astype(F32) * (GLA_DK ** -0.5)
    k = k_ref[...].astype(F32)
    outs = _gla_chunk(q, k, v_ref[...], g, st_ref, c_ref, mats_ref, masks_ref,
                      heads=heads, direction="bwd", want_out=True)
    gn = gn_ref[...]
    for h, ob in enumerate(outs):
        vs = slice(h * GLA_DV, (h + 1) * GLA_DV)
        o = _rms(of_ref[:, vs] + ob, gn)
        o_ref[:, vs] = (o * jax.nn.silu(gog_ref[:, vs].astype(F32))).astype(o_ref.dtype)


def _gla(proj_ac, proj_b, km, vm, lrm, wf, bf, wb, bb, gn, *, heads, n_meta):
    M = proj_ac.shape[0]
    C = GLA_CHUNK
    nch = M // C
    qw, vw = heads * GLA_DK, heads * GLA_DV
    const2 = lambda n: (0, 0)
    const3 = lambda n: (0, 0, 0)
    scratch = [pltpu.VMEM((heads, GLA_DV, GLA_DK), F32), pltpu.VMEM((C, qw), F32)]
    mats_f, masks_f = _gla_consts("fwd")
    mats_b, masks_b = _gla_consts("bwd")

    xb = lambda n: jnp.maximum(n - 1, 0)
    o_f = pl.pallas_call(
        functools.partial(_gla_fwd_kernel, heads=heads, n_meta=n_meta),
        grid=(nch + 1,),
        in_specs=[
            pl.BlockSpec((C, qw), lambda n: (xb(n), 0)),
            pl.BlockSpec((C, qw), lambda n: (xb(n), 1)),
            pl.BlockSpec((C, vw), lambda n: (xb(n), 1)),
            pl.BlockSpec((C, LANES), lambda n: (xb(n), 0)),
            pl.BlockSpec(km.shape, const2),
            pl.BlockSpec(vm.shape, const2),
            pl.BlockSpec((C, LANES), const2),
            pl.BlockSpec(wf.shape, const2),
            pl.BlockSpec(bf.shape, const2),
            pl.BlockSpec(mats_f.shape, const2),
            pl.BlockSpec(masks_f.shape, const3),
        ],
        out_specs=pl.BlockSpec((C, vw), lambda n: (xb(n), 0)),
        out_shape=jax.ShapeDtypeStruct((M, vw), F32),
        scratch_shapes=scratch,
        compiler_params=_cparams(("arbitrary",)),
        name="gla_fwd",
    )(proj_ac, proj_ac, proj_ac, proj_b, km, vm, lrm, wf, bf, mats_f, masks_f)

    rb = lambda n: nch - 1 - n
    return pl.pallas_call(
        functools.partial(_gla_bwd_kernel, heads=heads),
        grid=(nch,),
        in_specs=[
            pl.BlockSpec((C, qw), lambda n: (rb(n), 0)),
            pl.BlockSpec((C, qw), lambda n: (rb(n), 1)),
            pl.BlockSpec((C, vw), lambda n: (rb(n), 1)),
            pl.BlockSpec((C, vw), lambda n: (rb(n), 2)),
            pl.BlockSpec((C, LANES), lambda n: (rb(n), 1)),
            pl.BlockSpec((C, vw), lambda n: (rb(n), 0)),
            pl.BlockSpec(wb.shape, const2),
            pl.BlockSpec(bb.shape, const2),
            pl.BlockSpec(gn.shape, const2),
            pl.BlockSpec(mats_b.shape, const2),
            pl.BlockSpec(masks_b.shape, const3),
        ],
        out_specs=pl.BlockSpec((C, vw), lambda n: (rb(n), 0)),
        out_shape=jax.ShapeDtypeStruct((M, vw), BF16),
        scratch_shapes=scratch,
        compiler_params=_cparams(("arbitrary",)),
        name="gla_bwd",
    )(proj_ac, proj_ac, proj_ac, proj_ac, proj_b, o_f, wb, bb, gn, mats_b, masks_b)


def _merge_kernel(og_ref, om_ref, wg_ref, wm_ref, ga_ref, gb_ref, o_ref):
    yg = jnp.dot(og_ref[...], wg_ref[...], preferred_element_type=F32)
    ym = jnp.dot(om_ref[...], wm_ref[...], preferred_element_type=F32)
    ga = jax.nn.sigmoid(ga_ref[...].astype(F32))
    gb = jax.nn.sigmoid(gb_ref[...].astype(F32))
    o_ref[...] = (ga * yg + gb * ym).astype(o_ref.dtype)


def _merge(og, om, wg, wm, gates, *, tm=1024, tn=1024):
    M, D = og.shape[0], wg.shape[1]
    tm, tn = _tile(M, tm), _tile(D, tn)
    gb_blk = D // tn
    return pl.pallas_call(
        _merge_kernel,
        grid=(M // tm, D // tn),
        in_specs=[
            pl.BlockSpec((tm, og.shape[1]), lambda i, j: (i, 0)),
            pl.BlockSpec((tm, om.shape[1]), lambda i, j: (i, 0)),
            pl.BlockSpec((wg.shape[0], tn), lambda i, j: (0, j)),
            pl.BlockSpec((wm.shape[0], tn), lambda i, j: (0, j)),
            pl.BlockSpec((tm, tn), lambda i, j: (i, j)),
            pl.BlockSpec((tm, tn), lambda i, j: (i, j + gb_blk)),
        ],
        out_specs=pl.BlockSpec((tm, tn), lambda i, j: (i, j)),
        out_shape=jax.ShapeDtypeStruct((M, D), BF16),
        compiler_params=_cparams(("arbitrary", "arbitrary")),
        name="merge_out_proj",
    )(og, om, wg, wm, gates, gates)


def _rmm_kernel(a_ref, b_ref, r_ref, o_ref, *acc, nk):
    part = jnp.dot(a_ref[...], b_ref[...], preferred_element_type=F32)
    if nk == 1:
        o_ref[...] = r_ref[...] + part
        return
    acc_ref, = acc
    kk = pl.program_id(2)

    @pl.when(kk == 0)
    def _():
        acc_ref[...] = part

    @pl.when(kk > 0)
    def _():
        acc_ref[...] += part

    @pl.when(kk == nk - 1)
    def _():
        o_ref[...] = r_ref[...] + acc_ref[...]


def _res_matmul(a, b, res, *, tm=1024, tn=1024, tk=4096, name):
    M, K = a.shape
    N = b.shape[1]
    tm, tn, tk = _tile(M, tm), _tile(N, tn), _tile(K, tk)
    nk = K // tk
    return pl.pallas_call(
        functools.partial(_rmm_kernel, nk=nk),
        grid=(M // tm, N // tn, nk),
        in_specs=[
            pl.BlockSpec((tm, tk), lambda i, j, k: (i, k)),
            pl.BlockSpec((tk, tn), lambda i, j, k: (k, j)),
            pl.BlockSpec((tm, tn), lambda i, j, k: (i, j)),
        ],
        out_specs=pl.BlockSpec((tm, tn), lambda i, j, k: (i, j)),
        out_shape=jax.ShapeDtypeStruct((M, N), F32),
        scratch_shapes=[pltpu.VMEM((tm, tn), F32)] if nk > 1 else [],
        compiler_params=_cparams(("arbitrary", "arbitrary", "arbitrary")),
        name=name,
    )(a, b, res)


def _pad_cols(w, width):
    return jnp.pad(w, ((0, 0), (0, width - w.shape[1])))


def kernel(x, meta_tokens, ln1, w_in, gla_wf, gla_bf, gla_wb, gla_bb, gla_norm, q_norm, w_uq, kv_norm, w_ukv,
           w_gla_out, w_mla_out, w_o, ln2, w_ff1, w_ff2, final_norm):
    assert x.shape[0] == 1 and ln1.shape[0] == 1, "one sequence, one layer"
    S, D = x.shape[1], x.shape[2]
    n_meta = meta_tokens.shape[0]
    rank = gla_wf.shape[1]
    qk_w = gla_wf.shape[2]
    hg = qk_w // GLA_DK
    v_w = hg * GLA_DV
    q_rank, kv_rank = w_uq.shape[1], w_ukv.shape[1]
    hm = w_uq.shape[2] // (MLA_NOPE + MLA_ROPE)
    half = MLA_ROPE // 2
    assert n_meta <= GLA_CHUNK and rank <= LANES and hm % 2 == 0 and S % GLA_CHUNK == 0

    wt = w_in[0].T
    o_gog_end = 2 * qk_w + 2 * v_w
    o_lrb = o_gog_end + rank
    o_cq = o_lrb + rank
    o_ckv = o_cq + q_rank
    o_kr = o_ckv + kv_rank
    o_ga = o_kr + MLA_ROPE
    pad_rows = lambda a: jnp.pad(a, ((0, LANES - a.shape[0]), (0, 0)))
    wt_b = jnp.concatenate([
        pad_rows(wt[o_gog_end:o_lrb]), pad_rows(wt[o_lrb:o_cq]),
        pad_rows(wt[o_kr:o_kr + half]), pad_rows(wt[o_kr + half:o_ga]),
        wt[o_ckv:o_kr], wt[o_cq:o_ckv]], axis=0)
    kr_blk = 1
    ckv_blk = (4 * LANES) // kv_rank
    cq_blk = (4 * LANES + kv_rank) // q_rank
    assert (4 * LANES) % kv_rank == 0 and (4 * LANES + kv_rank) % q_rank == 0

    wq = w_uq[0].reshape(q_rank, hm, MLA_NOPE + MLA_ROPE)
    wq = jnp.concatenate([wq[:, :, :MLA_NOPE].reshape(q_rank, -1), wq[:, :, MLA_NOPE:].reshape(q_rank, -1)],
                         axis=1).astype(BF16)
    wkv = w_ukv[0].astype(BF16)
    wf = jnp.pad(gla_wf[0], ((0, LANES - rank), (0, 0))).astype(BF16)
    wb = jnp.pad(gla_wb[0], ((0, LANES - rank), (0, 0))).astype(BF16)

    inv_freq = ROPE_THETA ** (-jnp.arange(0, MLA_ROPE, 2, dtype=F32) / MLA_ROPE)
    ang = jnp.arange(n_meta + S, dtype=F32)[:, None] * inv_freq[None, :]
    cos, sin = jnp.cos(ang), jnp.sin(ang)
    cos_x, sin_x = cos[n_meta:], sin[n_meta:]
    cos_xp, sin_xp = _pad_cols(cos_x, LANES), _pad_cols(sin_x, LANES)
    mrows = GLA_CHUNK
    padm = lambda a: jnp.pad(a, ((0, mrows - n_meta), (0, LANES - a.shape[1])))
    cos_mp, sin_mp = padm(cos[:n_meta]), padm(sin[:n_meta])

    xs = x[0]
    xm = jnp.pad(meta_tokens.astype(F32), ((0, mrows - n_meta), (0, 0)))
    g1 = ln1[0][None, :]

    xn = _rms_rows(xs, g1, out_dtype=BF16, name="ln1_x")
    xmn = _rms_rows(xm, g1, out_dtype=BF16, name="ln1_meta")
    proj_a = _ws_matmul(xn, wt, col0=0, n_out=o_gog_end, out_dtype=BF16, tm=1024, tn=512, name="in_proj_a")
    gates = _ws_matmul(xn, wt, col0=o_ga, n_out=2 * D, out_dtype=BF16, tm=1024, tn=512, name="in_proj_gates")
    nb = wt_b.shape[0]
    proj_b = _ws_matmul(xn, wt_b, col0=0, n_out=nb, out_dtype=F32, tm=1024, tn=512, name="in_proj_b")
    proj_b_m = _ws_matmul(xmn, wt_b, col0=0, n_out=nb, out_dtype=F32, tm=mrows, tn=512, name="in_proj_b_meta")
    kv_m = _ws_matmul(xmn, wt, col0=qk_w, n_out=qk_w + v_w, out_dtype=BF16, tm=mrows, tn=512,
                      name="in_proj_kv_meta")

    og = _gla(proj_a, proj_b, kv_m[:, :qk_w], kv_m[:, qk_w:], proj_b_m[:, :LANES],
              wf, gla_bf, wb, gla_bb, gla_norm, heads=hg, n_meta=n_meta)

    qt = _q_up(proj_b, cq_blk, q_norm, wq, cos_x.T, sin_x.T, heads=hm)
    gkv = kv_norm
    k_x, vt_x = _kv_up(proj_b, gkv, wkv, cos_xp, sin_xp, heads=hm, ckv_blk=ckv_blk, kr_blk=kr_blk,
                       tm=_tile(S, 512))
    k_m, vt_m = _kv_up(proj_b_m, gkv, wkv, cos_mp, sin_mp, heads=hm, ckv_blk=ckv_blk, kr_blk=kr_blk, tm=mrows)
    om, (wgo, wmo, wo, w1, w2) = _flash(qt, k_x, vt_x, k_m, vt_m,
                                        (w_gla_out[0], w_mla_out[0], w_o[0], w_ff1[0], w_ff2[0]), n_meta=n_meta)

    merged = _merge(og, om, wgo, wmo, gates)
    h1 = _res_matmul(merged, wo, xs, name="o_proj")
    hf = _norm_matmul(h1, ln2[0][None, :], w1, out_dtype=BF16, act="relu2", name="ffn_up")
    h2 = _res_matmul(hf, w2, h1, tk=2048, name="ffn_down")
    return _rms_rows(h2, final_norm[None, :], out_dtype=F32, name="final_norm")[None]
```

```python
import functools
import math

import jax
import jax.numpy as jnp
import numpy as np
from jax import lax
from jax.experimental import pallas as pl
from jax.experimental.pallas import tpu as pltpu

F32 = jnp.float32
BF16 = jnp.bfloat16

GLA_DK = 128
GLA_DV = 256
GLA_GATE_NORM = 16.0
MLA_NOPE = 128
MLA_ROPE = 64
MLA_V = 128
ROPE_THETA = 10000.0
EPS = 1e-6

LANES = 128
GLA_CHUNK = 128
GLA_LEVELS = (64, 32, 16, 8, 4, 2, 1)
NEG_BIG = -1e30
VMEM_LIMIT = 56 * 1024 * 1024


def _cparams(sem):
    return pltpu.CompilerParams(dimension_semantics=sem, vmem_limit_bytes=VMEM_LIMIT)


def _tile(n, pref):
    t = min(n, pref)
    while n % t:
        t //= 2
    return t


def _rms(a, g, eps=EPS):
    ms = jnp.mean(a * a, axis=-1, keepdims=True)
    return a * lax.rsqrt(ms + eps) * g


def _nmm_kernel(a_ref, g_ref, b_ref, o_ref, an_ref, *, act):
    @pl.when(pl.program_id(1) == 0)
    def _():
        an_ref[...] = _rms(a_ref[...], g_ref[...]).astype(BF16)

    y = jnp.dot(an_ref[...], b_ref[...], preferred_element_type=F32)
    if act == "relu2":
        y = jnp.square(jnp.maximum(y, 0.0))
    o_ref[...] = y.astype(o_ref.dtype)


def _norm_matmul(a, g, b, *, out_dtype, tm=512, tn=1024, act=None, name):
    M, K = a.shape
    N = b.shape[1]
    tm, tn = _tile(M, tm), _tile(N, tn)
    return pl.pallas_call(
        functools.partial(_nmm_kernel, act=act),
        grid=(M // tm, N // tn),
        in_specs=[
            pl.BlockSpec((tm, K), lambda i, j: (i, 0)),
            pl.BlockSpec((1, K), lambda i, j: (0, 0)),
            pl.BlockSpec((K, tn), lambda i, j: (0, j)),
        ],
        out_specs=pl.BlockSpec((tm, tn), lambda i, j: (i, j)),
        out_shape=jax.ShapeDtypeStruct((M, N), out_dtype),
        scratch_shapes=[pltpu.VMEM((tm, K), BF16)],
        compiler_params=_cparams(("arbitrary", "arbitrary")),
        name=name,
    )(a, g, b)


def _rms_rows_kernel(x_ref, g_ref, o_ref):
    o_ref[...] = _rms(x_ref[...], g_ref[...]).astype(o_ref.dtype)


def _rms_rows(x, g, *, out_dtype, tm=256, name):
    M, D = x.shape
    tm = _tile(M, tm)
    return pl.pallas_call(
        _rms_rows_kernel,
        grid=(M // tm,),
        in_specs=[pl.BlockSpec((tm, D), lambda i: (i, 0)), pl.BlockSpec((1, D), lambda i: (0, 0))],
        out_specs=pl.BlockSpec((tm, D), lambda i: (i, 0)),
        out_shape=jax.ShapeDtypeStruct((M, D), out_dtype),
        compiler_params=_cparams(("arbitrary",)),
        name=name,
    )(x, g)


def _ws_kernel(*refs, shift):
    a_ref, w_ref = refs[0], refs[1]
    w2_ref = refs[2] if shift is not None else None
    o_ref, wb_ref = refs[-2], refs[-1]
    K, tn = wb_ref.shape

    @pl.when(pl.program_id(1) == 0)
    def _():
        kc = _tile(K, 512)
        for r in range(0, K, kc):
            if shift is None:
                blk = w_ref[:, r:r + kc]
            else:
                blk = jnp.concatenate([w_ref[shift:, r:r + kc], w2_ref[:shift, r:r + kc]], axis=0)
            wb_ref[r:r + kc, :] = blk.T.astype(BF16)

    o_ref[...] = jnp.dot(a_ref[...], wb_ref[...], preferred_element_type=F32).astype(o_ref.dtype)


def _ws_matmul(a, wt, *, col0, n_out, out_dtype, tm, tn, name):
    M, K = a.shape
    shift = col0 % LANES
    base = col0 - shift
    tm, tn = _tile(M, tm), _tile(math.gcd(n_out, base), tn)
    nj, ni = n_out // tn, M // tm
    in_specs = [pl.BlockSpec((tm, K), lambda j, i: (i, 0)),
                pl.BlockSpec((tn, K), lambda j, i: (j + base // tn, 0))]
    operands = [a, wt]
    if shift:
        in_specs.append(pl.BlockSpec((LANES, K), lambda j, i: ((j + 1) * (tn // LANES) + base // LANES, 0)))
        operands.append(wt)
    return pl.pallas_call(
        functools.partial(_ws_kernel, shift=shift if shift else None),
        grid=(nj, ni),
        in_specs=in_specs,
        out_specs=pl.BlockSpec((tm, tn), lambda j, i: (i, j)),
        out_shape=jax.ShapeDtypeStruct((M, n_out), out_dtype),
        scratch_shapes=[pltpu.VMEM((K, tn), BF16)],
        compiler_params=_cparams(("arbitrary", "arbitrary")),
        name=name,
    )(*operands)


def _q_up_kernel(cq_ref, g_ref, w_ref, cos_ref, sin_ref, qt_ref, *, heads, scale):
    cn = _rms(cq_ref[...], g_ref[...]).astype(BF16)
    y = jnp.dot(cn, w_ref[...], preferred_element_type=F32) * scale
    c = cos_ref[...]
    s = sin_ref[...]
    half = MLA_ROPE // 2
    for h in range(heads):
        qt_ref[h, 0:MLA_NOPE, :] = y[:, h * MLA_NOPE:(h + 1) * MLA_NOPE].T.astype(BF16)
    rope0 = heads * MLA_NOPE
    for p in range(heads // 2):
        blk = y[:, rope0 + p * LANES: rope0 + (p + 1) * LANES].T
        for u in range(2):
            x1 = blk[u * MLA_ROPE: u * MLA_ROPE + half]
            x2 = blk[u * MLA_ROPE + half: (u + 1) * MLA_ROPE]
            h = 2 * p + u
            qt_ref[h, MLA_NOPE:MLA_NOPE + half, :] = (x1 * c - x2 * s).astype(BF16)
            qt_ref[h, MLA_NOPE + half:MLA_NOPE + MLA_ROPE, :] = (x1 * s + x2 * c).astype(BF16)
    pad = qt_ref.shape[1] - MLA_NOPE - MLA_ROPE
    for h in range(heads):
        qt_ref[h, MLA_NOPE + MLA_ROPE:, :] = jnp.zeros((pad, qt_ref.shape[2]), BF16)


def _q_up(proj_b, cq_blk, g, w, cos_t, sin_t, *, heads, tm=512):
    M = proj_b.shape[0]
    R = g.shape[1]
    tm = _tile(M, tm)
    kq = 2 * LANES
    return pl.pallas_call(
        functools.partial(_q_up_kernel, heads=heads, scale=float((MLA_NOPE + MLA_ROPE) ** -0.5 * math.log2(math.e))),
        grid=(M // tm,),
        in_specs=[
            pl.BlockSpec((tm, R), lambda i: (i, cq_blk)),
            pl.BlockSpec((1, R), lambda i: (0, 0)),
            pl.BlockSpec(w.shape, lambda i: (0, 0)),
            pl.BlockSpec((MLA_ROPE // 2, tm), lambda i: (0, i)),
            pl.BlockSpec((MLA_ROPE // 2, tm), lambda i: (0, i)),
        ],
        out_specs=pl.BlockSpec((heads, kq, tm), lambda i: (0, 0, i)),
        out_shape=jax.ShapeDtypeStruct((heads, kq, M), BF16),
        compiler_params=_cparams(("arbitrary",)),
        name="mla_q_up",
    )(proj_b, g, w, cos_t, sin_t)


def _kv_up_kernel(ckv_ref, kr_ref, g_ref, w_ref, cos_ref, sin_ref, k_ref, vt_ref, *, heads):
    cn = _rms(ckv_ref[...], g_ref[...]).astype(BF16)
    y = jnp.dot(cn, w_ref[...], preferred_element_type=F32)
    x1 = kr_ref[:, 0:LANES]
    x2 = kr_ref[:, LANES:2 * LANES]
    c = cos_ref[...]
    s = sin_ref[...]
    kpe = (x1 * c - x2 * s) + pltpu.roll(x1 * s + x2 * c, MLA_ROPE // 2, axis=1)
    kpe = kpe.astype(BF16)
    hw = MLA_NOPE + MLA_V
    for h in range(heads):
        k_ref[h, :, 0:MLA_NOPE] = y[:, h * hw: h * hw + MLA_NOPE].astype(BF16)
        k_ref[h, :, MLA_NOPE:] = kpe
        vt_ref[h, 0] = y[:, h * hw + MLA_NOPE:(h + 1) * hw].T.astype(BF16)


def _kv_up(proj_b, g, w, cos_p, sin_p, *, heads, ckv_blk, kr_blk, tm):
    M = proj_b.shape[0]
    R = g.shape[1]
    return pl.pallas_call(
        functools.partial(_kv_up_kernel, heads=heads),
        grid=(M // tm,),
        in_specs=[
            pl.BlockSpec((tm, R), lambda i: (i, ckv_blk)),
            pl.BlockSpec((tm, 2 * LANES), lambda i: (i, kr_blk)),
            pl.BlockSpec((1, R), lambda i: (0, 0)),
            pl.BlockSpec(w.shape, lambda i: (0, 0)),
            pl.BlockSpec((tm, LANES), lambda i: (i, 0)),
            pl.BlockSpec((tm, LANES), lambda i: (i, 0)),
        ],
        out_specs=[
            pl.BlockSpec((heads, tm, 2 * LANES), lambda i: (0, i, 0)),
            pl.BlockSpec((heads, 1, MLA_V, tm), lambda i: (0, i, 0, 0)),
        ],
        out_shape=[
            jax.ShapeDtypeStruct((heads, M, 2 * LANES), BF16),
            jax.ShapeDtypeStruct((heads, M // tm, MLA_V, tm), BF16),
        ],
        compiler_params=_cparams(("arbitrary",)),
        name="mla_kv_up",
    )(proj_b, proj_b, g, w, cos_p, sin_p)


def _flash_kernel(*refs, nkb, tk, n_meta, streams, n_cast):
    qt_ref, k_ref, vt_ref, km_ref, vtm_ref = refs[:5]
    cast_in = refs[5:5 + n_cast]
    o_ref = refs[5 + n_cast]
    cast_out = refs[6 + n_cast:6 + 2 * n_cast]
    s_ref, p_ref, acc_ref = refs[6 + 2 * n_cast:]
    for ci, co in zip(cast_in, cast_out):
        co[...] = ci[...].astype(BF16)
    ts = qt_ref.shape[1] // streams
    strip = min(tk, 64)
    cols = [slice(c * ts, (c + 1) * ts) for c in range(streams)]

    def scores(c, j, slot):
        s_ref[c, slot] = jnp.dot(k_ref[j * tk:(j + 1) * tk, :], qt_ref[:, cols[c]], preferred_element_type=F32)

    def softmax(c, slot, m, l):
        mx = jnp.full((8, ts), NEG_BIG, F32)
        for r in range(0, tk, strip):
            mx = jnp.maximum(mx, jnp.max(s_ref[c, slot, r:r + strip, :].reshape(strip // 8, 8, ts), axis=0))
        m_new = jnp.maximum(m, jnp.max(mx, axis=0, keepdims=True))
        psum = jnp.zeros((8, ts), F32)
        for r in range(0, tk, strip):
            p = jnp.exp2(s_ref[c, slot, r:r + strip, :] - m_new)
            psum = psum + jnp.sum(p.reshape(strip // 8, 8, ts), axis=0)
            p_ref[c, slot, r:r + strip, :] = p.astype(BF16)
        alpha = jnp.exp2(m - m_new)
        return m_new, alpha * l + jnp.sum(psum, axis=0, keepdims=True), alpha

    def values(c, j, slot, alpha):
        acc_ref[c] = alpha * acc_ref[c] + jnp.dot(vt_ref[j], p_ref[c, slot], preferred_element_type=F32)

    stats = []
    for c in range(streams):
        scores(c, 0, 0)
        s_m = jnp.dot(km_ref[...], qt_ref[:, cols[c]], preferred_element_type=F32)
        rows = lax.broadcasted_iota(jnp.int32, s_m.shape, 0)
        s_m = jnp.where(rows < n_meta, s_m, NEG_BIG)
        m = jnp.max(s_m, axis=0, keepdims=True)
        p_m = jnp.exp2(s_m - m)
        l = jnp.sum(p_m, axis=0, keepdims=True)
        acc_ref[c] = jnp.dot(vtm_ref[0], p_m.astype(BF16), preferred_element_type=F32)
        stats.append((m, l, None))
    for j in range(nkb):
        for c in range(streams):
            m, l, alpha = stats[c]
            if j + 1 < nkb:
                scores(c, j + 1, (j + 1) % 2)
            if j > 0:
                values(c, j - 1, (j - 1) % 2, alpha)
            stats[c] = softmax(c, j % 2, m, l)
    for c in range(streams):
        m, l, alpha = stats[c]
        values(c, nkb - 1, (nkb - 1) % 2, alpha)
        o_ref[cols[c], :] = (acc_ref[c] / l).T.astype(o_ref.dtype)


def _cast_blocks(rows, steps):
    d = max(1, min(rows // 16, steps))
    while (rows // 16) % d:
        d -= 1
    return d


def _flash(qt, k, vt, km, vtm, casts, *, n_meta, tq=1024, streams=4):
    heads, kq, M = qt.shape
    nkb, tk = vt.shape[1], vt.shape[3]
    tq = _tile(M, tq)
    tkm = km.shape[1]
    ts = tq // streams
    nq = M // tq
    cast_specs = []
    for c in casts:
        nblk = _cast_blocks(c.shape[0], heads * nq)
        cast_specs.append(pl.BlockSpec((c.shape[0] // nblk, c.shape[1]),
                                       lambda h, i, nblk=nblk: (jnp.minimum(h * nq + i, nblk - 1), 0)))
    res = pl.pallas_call(
        functools.partial(_flash_kernel, nkb=nkb, tk=tk, n_meta=n_meta, streams=streams, n_cast=len(casts)),
        grid=(heads, nq),
        in_specs=[
            pl.BlockSpec((None, kq, tq), lambda h, i: (h, 0, i)),
            pl.BlockSpec((None, M, kq), lambda h, i: (h, 0, 0)),
            pl.BlockSpec((None, nkb, MLA_V, tk), lambda h, i: (h, 0, 0, 0)),
            pl.BlockSpec((None, tkm, kq), lambda h, i: (h, 0, 0)),
            pl.BlockSpec((None, 1, MLA_V, tkm), lambda h, i: (h, 0, 0, 0)),
        ] + cast_specs,
        out_specs=[pl.BlockSpec((tq, MLA_V), lambda h, i: (i, h))] + cast_specs,
        out_shape=[jax.ShapeDtypeStruct((M, heads * MLA_V), BF16)]
        + [jax.ShapeDtypeStruct(c.shape, BF16) for c in casts],
        scratch_shapes=[pltpu.VMEM((streams, 2, tk, ts), F32), pltpu.VMEM((streams, 2, tk, ts), BF16),
                        pltpu.VMEM((streams, MLA_V, ts), F32)],
        compiler_params=_cparams(("arbitrary", "arbitrary")),
        name="mla_flash",
    )(qt, k, vt, km, vtm, *casts)
    return res[0], tuple(res[1:])


def _gla_consts(direction):
    C = GLA_CHUNK
    r = np.arange(C)
    if direction == "fwd":
        cum = (r[None, :] <= r[:, None])
    else:
        cum = (r[None, :] >= r[:, None])
    eq2 = np.zeros((C, C), np.float32)
    ek2 = np.zeros((C, C), np.float32)
    for t in range(C):
        pos = t % 4
        if direction == "fwd":
            if pos == 2:
                eq2[t, t] = 1
            elif pos == 3:
                eq2[t, t] = 1
                eq2[t, t - 1] = 1
            elif pos == 0:
                ek2[t, t + 1] = 1
        else:
            if pos == 1:
                eq2[t, t] = 1
            elif pos == 0:
                eq2[t, t] = 1
                eq2[t, t + 1] = 1
            elif pos == 3:
                ek2[t, t - 1] = 1
    mats = np.concatenate([cum.astype(np.float32), eq2, ek2], axis=0)
    masks = np.zeros((6, C, LANES), np.float32)
    for li, m in enumerate((4, 2, 1)):
        upper = (r % (2 * m)) >= m
        qrows = upper if direction == "fwd" else ~upper
        masks[2 * li] = np.where(qrows, 0.0, NEG_BIG)[:, None]
        masks[2 * li + 1] = np.where(~qrows, 0.0, NEG_BIG)[:, None]
    return jnp.asarray(mats, BF16), jnp.asarray(masks, F32)


def _split3_dot(mat, x):
    x1 = x.astype(BF16)
    r1 = x - x1.astype(F32)
    x2 = r1.astype(BF16)
    x3 = (r1 - x2.astype(F32)).astype(BF16)
    return (jnp.dot(mat, x1, preferred_element_type=F32)
            + jnp.dot(mat, x2, preferred_element_type=F32)
            + jnp.dot(mat, x3, preferred_element_type=F32))


def _log_decay(lr, w_ref, b_ref):
    z = jnp.dot(lr.astype(BF16), w_ref[...], preferred_element_type=F32) + b_ref[...]
    return jax.nn.log_sigmoid(z) * (math.log2(math.e) / GLA_GATE_NORM)


def _gla_chunk(q, k, v, g, st_ref, c_ref, mats_ref, masks_ref, *, heads, direction, want_out):
    C = GLA_CHUNK
    fwd = direction == "fwd"
    e_all = _split3_dot(mats_ref[...], g)
    c_all = e_all[0:C]
    eq2_all = e_all[C:2 * C]
    ek2_all = e_all[2 * C:3 * C]
    c_ref[...] = c_all
    far = C - 1 if fwd else 0
    outs = []
    for h in range(heads):
        ls = slice(h * GLA_DK, (h + 1) * GLA_DK)
        vs = slice(h * GLA_DV, (h + 1) * GLA_DV)
        kh, gh, ch = k[:, ls], g[:, ls], c_all[:, ls]
        qh = q[:, ls] if want_out else None
        vh = v[:, vs]
        tot = c_ref[pl.ds(far, 1), ls]
        st = st_ref[h]
        if want_out:
            xor = lax.broadcasted_iota(jnp.int32, (C, C), 0) ^ lax.broadcasted_iota(jnp.int32, (C, C), 1)
            dg = jnp.sum(qh * kh, axis=-1, keepdims=True)
            att = jnp.where(xor == 0, dg, 0.0)
            qcat, kcat = [], []
            for m in GLA_LEVELS:
                if m >= 8:
                    qp, kp = [], []
                    for r0 in range(0, C, 2 * m):
                        lo = slice(r0, r0 + m)
                        hi = slice(r0 + m, r0 + 2 * m)
                        ref = c_ref[pl.ds(r0 + m - 1 if fwd else r0 + m, 1), ls]
                        qr, kr = (hi, lo) if fwd else (lo, hi)
                        qa = qh[qr] * jnp.exp2(ch[qr] - ref)
                        ka = kh[kr] * jnp.exp2(ref - ch[kr])
                        z = jnp.zeros((m, GLA_DK), F32)
                        qp += [z, qa] if fwd else [qa, z]
                        kp += [ka, z] if fwd else [z, ka]
                    qcat.append(jnp.concatenate(qp, axis=0).astype(BF16))
                    kcat.append(jnp.concatenate(kp, axis=0).astype(BF16))
                else:
                    li = (4, 2, 1).index(m)
                    mq = masks_ref[2 * li]
                    mk = masks_ref[2 * li + 1]
                    if m == 4:
                        ref = jnp.concatenate(
                            [jnp.broadcast_to(c_ref[pl.ds(r0 + 3 if fwd else r0 + 4, 1), ls], (8, GLA_DK))
                             for r0 in range(0, C, 8)], axis=0)
                        eq = ch - ref
                        ek = ref - ch
                    elif m == 2:
                        eq = eq2_all[:, ls]
                        ek = ek2_all[:, ls]
                    else:
                        eq = gh
                        ek = jnp.zeros_like(gh)
                    qcat.append((qh * jnp.exp2(eq + mq)).astype(BF16))
                    kcat.append((kh * jnp.exp2(ek + mk)).astype(BF16))
                pm = lax.dot_general(qcat.pop(), kcat.pop(), (((1,), (1,)), ((), ())),
                                     preferred_element_type=F32)
                att = att + (pm if 2 * m == C else jnp.where(xor < 2 * m, pm, 0.0))
            qs = (qh * jnp.exp2(ch)).astype(BF16)
            o = jnp.dot(att.astype(BF16), vh, preferred_element_type=F32)
            o = o + lax.dot_general(qs, st.astype(BF16), (((1,), (1,)), ((), ())),
                                    preferred_element_type=F32)
            outs.append(o)
        kd = (kh * jnp.exp2(tot - ch)).astype(BF16)
        ut = lax.dot_general(vh, kd, (((0,), (0,)), ((), ())), preferred_element_type=F32)
        st_ref[h] = st * jnp.exp2(tot) + ut
    return outs if want_out else None


def _gla_fwd_kernel(q_ref, k_ref, v_ref, lr_ref, km_ref, vm_ref, lrm_ref, w_ref, b_ref, mats_ref, masks_ref,
                    o_ref, st_ref, c_ref, *, heads, n_meta, cps):
    n = pl.program_id(0)
    common = dict(heads=heads, direction="fwd")

    @pl.when(n == 0)
    def _():
        st_ref[...] = jnp.zeros_like(st_ref)
        g = _log_decay(lrm_ref[...], w_ref, b_ref)
        rows = lax.broadcasted_iota(jnp.int32, g.shape, 0)
        g = jnp.where(rows < n_meta, g, 0.0)
        k = km_ref[...].astype(F32)
        _gla_chunk(None, k, vm_ref[...], g, st_ref, c_ref.at[0], mats_ref, masks_ref, want_out=False, **common)

    @pl.when(n > 0)
    def _():
        for u in range(cps):
            rows = slice(u * GLA_CHUNK, (u + 1) * GLA_CHUNK)
            g = _log_decay(lr_ref[rows, :], w_ref, b_ref)
            q = q_ref[rows, :].astype(F32) * (GLA_DK ** -0.5)
            k = k_ref[rows, :].astype(F32)
            outs = _gla_chunk(q, k, v_ref[rows, :], g, st_ref, c_ref.at[u], mats_ref, masks_ref,
                              want_out=True, **common)
            for h, o in enumerate(outs):
                o_ref[rows, h * GLA_DV:(h + 1) * GLA_DV] = o


def _gla_bwd_kernel(q_ref, k_ref, v_ref, gog_ref, lr_ref, of_ref, w_ref, b_ref, gn_ref, mats_ref, masks_ref,
                    o_ref, st_ref, c_ref, *, heads, cps):
    @pl.when(pl.program_id(0) == 0)
    def _():
        st_ref[...] = jnp.zeros_like(st_ref)

    gn = gn_ref[...]
    for u in reversed(range(cps)):
        rows = slice(u * GLA_CHUNK, (u + 1) * GLA_CHUNK)
        g = _log_decay(lr_ref[rows, :], w_ref, b_ref)
        q = q_ref[rows, :].astype(F32) * (GLA_DK ** -0.5)
        k = k_ref[rows, :].astype(F32)
        outs = _gla_chunk(q, k, v_ref[rows, :], g, st_ref, c_ref.at[u], mats_ref, masks_ref,
                          heads=heads, direction="bwd", want_out=True)
        for h, ob in enumerate(outs):
            vs = slice(h * GLA_DV, (h + 1) * GLA_DV)
            o = _rms(of_ref[rows, vs] + ob, gn)
            o_ref[rows, vs] = (o * jax.nn.silu(gog_ref[rows, vs].astype(F32))).astype(o_ref.dtype)


def _gla(proj_ac, proj_b, km, vm, lrm, wf, bf, wb, bb, gn, *, heads, n_meta, cps=4):
    M = proj_ac.shape[0]
    cps = cps if M % (cps * GLA_CHUNK) == 0 else 1
    C = cps * GLA_CHUNK
    nch = M // C
    qw, vw = heads * GLA_DK, heads * GLA_DV
    const2 = lambda n: (0, 0)
    const3 = lambda n: (0, 0, 0)
    scratch = [pltpu.VMEM((heads, GLA_DV, GLA_DK), F32), pltpu.VMEM((cps, GLA_CHUNK, qw), F32)]
    mats_f, masks_f = _gla_consts("fwd")
    mats_b, masks_b = _gla_consts("bwd")

    xb = lambda n: jnp.maximum(n - 1, 0)
    o_f = pl.pallas_call(
        functools.partial(_gla_fwd_kernel, heads=heads, n_meta=n_meta, cps=cps),
        grid=(nch + 1,),
        in_specs=[
            pl.BlockSpec((C, qw), lambda n: (xb(n), 0)),
            pl.BlockSpec((C, qw), lambda n: (xb(n), 1)),
            pl.BlockSpec((C, vw), lambda n: (xb(n), 1)),
            pl.BlockSpec((C, LANES), lambda n: (xb(n), 0)),
            pl.BlockSpec(km.shape, const2),
            pl.BlockSpec(vm.shape, const2),
            pl.BlockSpec(lrm.shape, const2),
            pl.BlockSpec(wf.shape, const2),
            pl.BlockSpec(bf.shape, const2),
            pl.BlockSpec(mats_f.shape, const2),
            pl.BlockSpec(masks_f.shape, const3),
        ],
        out_specs=pl.BlockSpec((C, vw), lambda n: (xb(n), 0)),
        out_shape=jax.ShapeDtypeStruct((M, vw), F32),
        scratch_shapes=scratch,
        compiler_params=_cparams(("arbitrary",)),
        name="gla_fwd",
    )(proj_ac, proj_ac, proj_ac, proj_b, km, vm, lrm, wf, bf, mats_f, masks_f)

    rb = lambda n: nch - 1 - n
    return pl.pallas_call(
        functools.partial(_gla_bwd_kernel, heads=heads, cps=cps),
        grid=(nch,),
        in_specs=[
            pl.BlockSpec((C, qw), lambda n: (rb(n), 0)),
            pl.BlockSpec((C, qw), lambda n: (rb(n), 1)),
            pl.BlockSpec((C, vw), lambda n: (rb(n), 1)),
            pl.BlockSpec((C, vw), lambda n: (rb(n), 2)),
            pl.BlockSpec((C, LANES), lambda n: (rb(n), 1)),
            pl.BlockSpec((C, vw), lambda n: (rb(n), 0)),
            pl.BlockSpec(wb.shape, const2),
            pl.BlockSpec(bb.shape, const2),
            pl.BlockSpec(gn.shape, const2),
            pl.BlockSpec(mats_b.shape, const2),
            pl.BlockSpec(masks_b.shape, const3),
        ],
        out_specs=pl.BlockSpec((C, vw), lambda n: (rb(n), 0)),
        out_shape=jax.ShapeDtypeStruct((M, vw), BF16),
        scratch_shapes=scratch,
        compiler_params=_cparams(("arbitrary",)),
        name="gla_bwd",
    )(proj_ac, proj_ac, proj_ac, proj_ac, proj_b, o_f, wb, bb, gn, mats_b, masks_b)


def _merge_kernel(og_ref, om_ref, wg_ref, wm_ref, ga_ref, gb_ref, o_ref):
    yg = jnp.dot(og_ref[...], wg_ref[...], preferred_element_type=F32)
    ym = jnp.dot(om_ref[...], wm_ref[...], preferred_element_type=F32)
    ga = jax.nn.sigmoid(ga_ref[...].astype(F32))
    gb = jax.nn.sigmoid(gb_ref[...].astype(F32))
    o_ref[...] = (ga * yg + gb * ym).astype(o_ref.dtype)


def _merge(og, om, wg, wm, gates, *, tm=1024, tn=1024):
    M, D = og.shape[0], wg.shape[1]
    tm, tn = _tile(M, tm), _tile(D, tn)
    gb_blk = D // tn
    return pl.pallas_call(
        _merge_kernel,
        grid=(M // tm, D // tn),
        in_specs=[
            pl.BlockSpec((tm, og.shape[1]), lambda i, j: (i, 0)),
            pl.BlockSpec((tm, om.shape[1]), lambda i, j: (i, 0)),
            pl.BlockSpec((wg.shape[0], tn), lambda i, j: (0, j)),
            pl.BlockSpec((wm.shape[0], tn), lambda i, j: (0, j)),
            pl.BlockSpec((tm, tn), lambda i, j: (i, j)),
            pl.BlockSpec((tm, tn), lambda i, j: (i, j + gb_blk)),
        ],
        out_specs=pl.BlockSpec((tm, tn), lambda i, j: (i, j)),
        out_shape=jax.ShapeDtypeStruct((M, D), BF16),
        compiler_params=_cparams(("arbitrary", "arbitrary")),
        name="merge_out_proj",
    )(og, om, wg, wm, gates, gates)


def _rmm_kernel(a_ref, b_ref, r_ref, o_ref, *acc, nk):
    def part():
        return jnp.dot(a_ref[...], b_ref[...], preferred_element_type=F32)

    if nk == 1:
        o_ref[...] = r_ref[...] + part()
        return
    acc_ref, = acc
    kk = pl.program_id(2)

    @pl.when(kk == 0)
    def _():
        acc_ref[...] = part()

    @pl.when((kk > 0) & (kk < nk - 1))
    def _():
        acc_ref[...] += part()

    @pl.when(kk == nk - 1)
    def _():
        o_ref[...] = r_ref[...] + (acc_ref[...] + part())


def _res_matmul(a, b, res, *, tm=1024, tn=1024, tk=4096, name):
    M, K = a.shape
    N = b.shape[1]
    tm, tn, tk = _tile(M, tm), _tile(N, tn), _tile(K, tk)
    nk = K // tk
    return pl.pallas_call(
        functools.partial(_rmm_kernel, nk=nk),
        grid=(M // tm, N // tn, nk),
        in_specs=[
            pl.BlockSpec((tm, tk), lambda i, j, k: (i, k)),
            pl.BlockSpec((tk, tn), lambda i, j, k: (k, j)),
            pl.BlockSpec((tm, tn), lambda i, j, k: (i, j)),
        ],
        out_specs=pl.BlockSpec((tm, tn), lambda i, j, k: (i, j)),
        out_shape=jax.ShapeDtypeStruct((M, N), F32),
        scratch_shapes=[pltpu.VMEM((tm, tn), F32)] if nk > 1 else [],
        compiler_params=_cparams(("arbitrary", "arbitrary", "arbitrary")),
        name=name,
    )(a, b, res)


def _pad_cols(w, width):
    return jnp.pad(w, ((0, 0), (0, width - w.shape[1])))


def kernel(x, meta_tokens, ln1, w_in, gla_wf, gla_bf, gla_wb, gla_bb, gla_norm, q_norm, w_uq, kv_norm, w_ukv,
           w_gla_out, w_mla_out, w_o, ln2, w_ff1, w_ff2, final_norm):
    assert x.shape[0] == 1 and ln1.shape[0] == 1, "one sequence, one layer"
    S, D = x.shape[1], x.shape[2]
    n_meta = meta_tokens.shape[0]
    rank = gla_wf.shape[1]
    qk_w = gla_wf.shape[2]
    hg = qk_w // GLA_DK
    v_w = hg * GLA_DV
    q_rank, kv_rank = w_uq.shape[1], w_ukv.shape[1]
    hm = w_uq.shape[2] // (MLA_NOPE + MLA_ROPE)
    half = MLA_ROPE // 2
    assert n_meta <= GLA_CHUNK and rank <= LANES and hm % 2 == 0 and S % GLA_CHUNK == 0

    wt = w_in[0].T
    o_gog_end = 2 * qk_w + 2 * v_w
    o_lrb = o_gog_end + rank
    o_cq = o_lrb + rank
    o_ckv = o_cq + q_rank
    o_kr = o_ckv + kv_rank
    o_ga = o_kr + MLA_ROPE
    pad_rows = lambda a: jnp.pad(a, ((0, LANES - a.shape[0]), (0, 0)))
    wt_b = jnp.concatenate([
        pad_rows(wt[o_gog_end:o_lrb]), pad_rows(wt[o_lrb:o_cq]),
        pad_rows(wt[o_kr:o_kr + half]), pad_rows(wt[o_kr + half:o_ga]),
        wt[o_ckv:o_kr], wt[o_cq:o_ckv]], axis=0)
    kr_blk = 1
    ckv_blk = (4 * LANES) // kv_rank
    cq_blk = (4 * LANES + kv_rank) // q_rank
    assert (4 * LANES) % kv_rank == 0 and (4 * LANES + kv_rank) % q_rank == 0

    wq = w_uq[0].reshape(q_rank, hm, MLA_NOPE + MLA_ROPE)
    wq = jnp.concatenate([wq[:, :, :MLA_NOPE].reshape(q_rank, -1), wq[:, :, MLA_NOPE:].reshape(q_rank, -1)],
                         axis=1).astype(BF16)
    wkv = w_ukv[0].astype(BF16)
    wf = jnp.pad(gla_wf[0], ((0, LANES - rank), (0, 0))).astype(BF16)
    wb = jnp.pad(gla_wb[0], ((0, LANES - rank), (0, 0))).astype(BF16)

    inv_freq = ROPE_THETA ** (-jnp.arange(0, MLA_ROPE, 2, dtype=F32) / MLA_ROPE)
    ang = jnp.arange(n_meta + S, dtype=F32)[:, None] * inv_freq[None, :]
    cos, sin = jnp.cos(ang), jnp.sin(ang)
    cos_x, sin_x = cos[n_meta:], sin[n_meta:]
    cos_xp, sin_xp = _pad_cols(cos_x, LANES), _pad_cols(sin_x, LANES)
    mrows = GLA_CHUNK
    padm = lambda a: jnp.pad(a, ((0, mrows - n_meta), (0, LANES - a.shape[1])))
    cos_mp, sin_mp = padm(cos[:n_meta]), padm(sin[:n_meta])

    xs = x[0]
    xm = jnp.pad(meta_tokens.astype(F32), ((0, mrows - n_meta), (0, 0)))
    g1 = ln1[0][None, :]

    xn = _rms_rows(xs, g1, out_dtype=BF16, name="ln1_x")
    xmn = _rms_rows(xm, g1, out_dtype=BF16, name="ln1_meta")
    proj_a = _ws_matmul(xn, wt, col0=0, n_out=o_gog_end, out_dtype=BF16, tm=1024, tn=512, name="in_proj_a")
    gates = _ws_matmul(xn, wt, col0=o_ga, n_out=2 * D, out_dtype=BF16, tm=1024, tn=512, name="in_proj_gates")
    nb = wt_b.shape[0]
    proj_b = _ws_matmul(xn, wt_b, col0=0, n_out=nb, out_dtype=F32, tm=1024, tn=512, name="in_proj_b")
    proj_b_m = _ws_matmul(xmn, wt_b, col0=0, n_out=nb, out_dtype=F32, tm=mrows, tn=512, name="in_proj_b_meta")
    kv_m = _ws_matmul(xmn, wt, col0=qk_w, n_out=qk_w + v_w, out_dtype=BF16, tm=mrows, tn=512,
                      name="in_proj_kv_meta")

    og = _gla(proj_a, proj_b, kv_m[:, :qk_w], kv_m[:, qk_w:], proj_b_m[:, :LANES],
              wf, gla_bf, wb, gla_bb, gla_norm, heads=hg, n_meta=n_meta)

    qt = _q_up(proj_b, cq_blk, q_norm, wq, cos_x.T, sin_x.T, heads=hm)
    gkv = kv_norm
    k_x, vt_x = _kv_up(proj_b, gkv, wkv, cos_xp, sin_xp, heads=hm, ckv_blk=ckv_blk, kr_blk=kr_blk,
                       tm=_tile(S, 512))
    k_m, vt_m = _kv_up(proj_b_m, gkv, wkv, cos_mp, sin_mp, heads=hm, ckv_blk=ckv_blk, kr_blk=kr_blk, tm=mrows)
    om, (wgo, wmo, wo, w1, w2) = _flash(qt, k_x, vt_x, k_m, vt_m,
                                        (w_gla_out[0], w_mla_out[0], w_o[0], w_ff1[0], w_ff2[0]), n_meta=n_meta)

    merged = _merge(og, om, wgo, wmo, gates)
    h1 = _res_matmul(merged, wo, xs, name="o_proj")
    hf = _norm_matmul(h1, ln2[0][None, :], w1, out_dtype=BF16, act="relu2", name="ffn_up")
    h2 = _res_matmul(hf, w2, h1, tk=2048, name="ffn_down")
    return _rms_rows(h2, final_norm[None, :], out_dtype=F32, name="final_norm")[None]
```

```python
import functools
import math

import jax
import jax.numpy as jnp
import numpy as np
from jax import lax
from jax.experimental import pallas as pl
from jax.experimental.pallas import tpu as pltpu

F32 = jnp.float32
BF16 = jnp.bfloat16

GLA_DK = 128
GLA_DV = 256
GLA_GATE_NORM = 16.0
MLA_NOPE = 128
MLA_ROPE = 64
MLA_V = 128
ROPE_THETA = 10000.0
EPS = 1e-6

LANES = 128
GLA_CHUNK = 128
GLA_LEVELS = (64, 32, 16, 8, 4, 2, 1)
NEG_BIG = -1e30
VMEM_LIMIT = 56 * 1024 * 1024


def _cparams(sem):
    return pltpu.CompilerParams(dimension_semantics=sem, vmem_limit_bytes=VMEM_LIMIT)


def _tile(n, pref):
    t = min(n, pref)
    while n % t:
        t //= 2
    return t


def _rms(a, g, eps=EPS):
    ms = jnp.mean(a * a, axis=-1, keepdims=True)
    return a * lax.rsqrt(ms + eps) * g


def _nmm_kernel(a_ref, g_ref, b_ref, o_ref, an_ref, *, act):
    def project(an):
        y = jnp.dot(an, b_ref[...], preferred_element_type=F32)
        if act == "relu2":
            y = jnp.square(jnp.maximum(y, 0.0))
        o_ref[...] = y.astype(o_ref.dtype)

    @pl.when(pl.program_id(1) == 0)
    def _():
        an = _rms(a_ref[...], g_ref[...]).astype(BF16)
        an_ref[...] = an
        project(an)

    @pl.when(pl.program_id(1) > 0)
    def _():
        project(an_ref[...])


def _norm_matmul(a, g, b, *, out_dtype, tm=512, tn=1024, act=None, name):
    M, K = a.shape
    N = b.shape[1]
    tm, tn = _tile(M, tm), _tile(N, tn)
    return pl.pallas_call(
        functools.partial(_nmm_kernel, act=act),
        grid=(M // tm, N // tn),
        in_specs=[
            pl.BlockSpec((tm, K), lambda i, j: (i, 0)),
            pl.BlockSpec((1, K), lambda i, j: (0, 0)),
            pl.BlockSpec((K, tn), lambda i, j: (0, j)),
        ],
        out_specs=pl.BlockSpec((tm, tn), lambda i, j: (i, j)),
        out_shape=jax.ShapeDtypeStruct((M, N), out_dtype),
        scratch_shapes=[pltpu.VMEM((tm, K), BF16)],
        compiler_params=_cparams(("arbitrary", "arbitrary")),
        name=name,
    )(a, g, b)


def _rms_rows_kernel(x_ref, g_ref, o_ref):
    o_ref[...] = _rms(x_ref[...], g_ref[...]).astype(o_ref.dtype)


def _rms_rows(x, g, *, out_dtype, tm=256, name):
    M, D = x.shape
    tm = _tile(M, tm)
    return pl.pallas_call(
        _rms_rows_kernel,
        grid=(M // tm,),
        in_specs=[pl.BlockSpec((tm, D), lambda i: (i, 0)), pl.BlockSpec((1, D), lambda i: (0, 0))],
        out_specs=pl.BlockSpec((tm, D), lambda i: (i, 0)),
        out_shape=jax.ShapeDtypeStruct((M, D), out_dtype),
        compiler_params=_cparams(("arbitrary",)),
        name=name,
    )(x, g)


def _ws_kernel(*refs, shift):
    a_ref, w_ref = refs[0], refs[1]
    w2_ref = refs[2] if shift is not None else None
    o_ref, wb_ref = refs[-2], refs[-1]
    K, tn = wb_ref.shape

    def project():
        o_ref[...] = jnp.dot(a_ref[...], wb_ref[...], preferred_element_type=F32).astype(o_ref.dtype)

    @pl.when(pl.program_id(1) == 0)
    def _():
        kc = _tile(K, 512)
        for r in range(0, K, kc):
            if shift is None:
                blk = w_ref[:, r:r + kc]
            else:
                blk = jnp.concatenate([w_ref[shift:, r:r + kc], w2_ref[:shift, r:r + kc]], axis=0)
            wb_ref[r:r + kc, :] = blk.T.astype(BF16)
        project()

    pl.when(pl.program_id(1) > 0)(project)


def _ws_matmul(a, wt, *, col0, n_out, out_dtype, tm, tn, name):
    M, K = a.shape
    shift = col0 % LANES
    base = col0 - shift
    tm, tn = _tile(M, tm), _tile(math.gcd(n_out, base), tn)
    nj, ni = n_out // tn, M // tm
    in_specs = [pl.BlockSpec((tm, K), lambda j, i: (i, 0)),
                pl.BlockSpec((tn, K), lambda j, i: (j + base // tn, 0))]
    operands = [a, wt]
    if shift:
        in_specs.append(pl.BlockSpec((LANES, K), lambda j, i: ((j + 1) * (tn // LANES) + base // LANES, 0)))
        operands.append(wt)
    return pl.pallas_call(
        functools.partial(_ws_kernel, shift=shift if shift else None),
        grid=(nj, ni),
        in_specs=in_specs,
        out_specs=pl.BlockSpec((tm, tn), lambda j, i: (i, j)),
        out_shape=jax.ShapeDtypeStruct((M, n_out), out_dtype),
        scratch_shapes=[pltpu.VMEM((K, tn), BF16)],
        compiler_params=_cparams(("arbitrary", "arbitrary")),
        name=name,
    )(*operands)


def _q_up_kernel(cq_ref, g_ref, w_ref, cos_ref, sin_ref, qt_ref, *, heads, scale):
    cn = _rms(cq_ref[...], g_ref[...]).astype(BF16)
    y = jnp.dot(cn, w_ref[...], preferred_element_type=F32) * scale
    c = cos_ref[...]
    s = sin_ref[...]
    half = MLA_ROPE // 2
    for h in range(heads):
        qt_ref[h, 0:MLA_NOPE, :] = y[:, h * MLA_NOPE:(h + 1) * MLA_NOPE].T.astype(BF16)
    rope0 = heads * MLA_NOPE
    for p in range(heads // 2):
        blk = y[:, rope0 + p * LANES: rope0 + (p + 1) * LANES].T
        for u in range(2):
            x1 = blk[u * MLA_ROPE: u * MLA_ROPE + half]
            x2 = blk[u * MLA_ROPE + half: (u + 1) * MLA_ROPE]
            h = 2 * p + u
            qt_ref[h, MLA_NOPE:MLA_NOPE + half, :] = (x1 * c - x2 * s).astype(BF16)
            qt_ref[h, MLA_NOPE + half:MLA_NOPE + MLA_ROPE, :] = (x1 * s + x2 * c).astype(BF16)
    pad = qt_ref.shape[1] - MLA_NOPE - MLA_ROPE
    for h in range(heads):
        qt_ref[h, MLA_NOPE + MLA_ROPE:, :] = jnp.zeros((pad, qt_ref.shape[2]), BF16)


def _q_up(proj_b, cq_blk, g, w, cos_t, sin_t, *, heads, tm=512):
    M = proj_b.shape[0]
    R = g.shape[1]
    tm = _tile(M, tm)
    kq = 2 * LANES
    return pl.pallas_call(
        functools.partial(_q_up_kernel, heads=heads, scale=float((MLA_NOPE + MLA_ROPE) ** -0.5 * math.log2(math.e))),
        grid=(M // tm,),
        in_specs=[
            pl.BlockSpec((tm, R), lambda i: (i, cq_blk)),
            pl.BlockSpec((1, R), lambda i: (0, 0)),
            pl.BlockSpec(w.shape, lambda i: (0, 0)),
            pl.BlockSpec((MLA_ROPE // 2, tm), lambda i: (0, i)),
            pl.BlockSpec((MLA_ROPE // 2, tm), lambda i: (0, i)),
        ],
        out_specs=pl.BlockSpec((heads, kq, tm), lambda i: (0, 0, i)),
        out_shape=jax.ShapeDtypeStruct((heads, kq, M), BF16),
        compiler_params=_cparams(("arbitrary",)),
        name="mla_q_up",
    )(proj_b, g, w, cos_t, sin_t)


def _kv_up_kernel(ckv_ref, kr_ref, g_ref, w_ref, cos_ref, sin_ref, k_ref, vt_ref, *, heads):
    cn = _rms(ckv_ref[...], g_ref[...]).astype(BF16)
    y = jnp.dot(cn, w_ref[...], preferred_element_type=F32)
    x1 = kr_ref[:, 0:LANES]
    x2 = kr_ref[:, LANES:2 * LANES]
    c = cos_ref[...]
    s = sin_ref[...]
    kpe = (x1 * c - x2 * s) + pltpu.roll(x1 * s + x2 * c, MLA_ROPE // 2, axis=1)
    kpe = kpe.astype(BF16)
    hw = MLA_NOPE + MLA_V
    for h in range(heads):
        k_ref[h, :, 0:MLA_NOPE] = y[:, h * hw: h * hw + MLA_NOPE].astype(BF16)
        k_ref[h, :, MLA_NOPE:] = kpe
        vt_ref[h, 0] = y[:, h * hw + MLA_NOPE:(h + 1) * hw].T.astype(BF16)


def _kv_up(proj_b, g, w, cos_p, sin_p, *, heads, ckv_blk, kr_blk, tm):
    M = proj_b.shape[0]
    R = g.shape[1]
    return pl.pallas_call(
        functools.partial(_kv_up_kernel, heads=heads),
        grid=(M // tm,),
        in_specs=[
            pl.BlockSpec((tm, R), lambda i: (i, ckv_blk)),
            pl.BlockSpec((tm, 2 * LANES), lambda i: (i, kr_blk)),
            pl.BlockSpec((1, R), lambda i: (0, 0)),
            pl.BlockSpec(w.shape, lambda i: (0, 0)),
            pl.BlockSpec((tm, LANES), lambda i: (i, 0)),
            pl.BlockSpec((tm, LANES), lambda i: (i, 0)),
        ],
        out_specs=[
            pl.BlockSpec((heads, tm, 2 * LANES), lambda i: (0, i, 0)),
            pl.BlockSpec((heads, 1, MLA_V, tm), lambda i: (0, i, 0, 0)),
        ],
        out_shape=[
            jax.ShapeDtypeStruct((heads, M, 2 * LANES), BF16),
            jax.ShapeDtypeStruct((heads, M // tm, MLA_V, tm), BF16),
        ],
        compiler_params=_cparams(("arbitrary",)),
        name="mla_kv_up",
    )(proj_b, proj_b, g, w, cos_p, sin_p)


def _flash_kernel(*refs, nkb, tk, n_meta, streams, n_cast):
    qt_ref, k_ref, vt_ref, km_ref, vtm_ref = refs[:5]
    cast_in = refs[5:5 + n_cast]
    o_ref = refs[5 + n_cast]
    cast_out = refs[6 + n_cast:6 + 2 * n_cast]
    s_ref, p_ref, acc_ref = refs[6 + 2 * n_cast:]
    for ci, co in zip(cast_in, cast_out):
        co[...] = ci[...].astype(BF16)
    ts = qt_ref.shape[1] // streams
    strip = min(tk, 64)
    cols = [slice(c * ts, (c + 1) * ts) for c in range(streams)]

    def scores(c, j, slot):
        s_ref[c, slot] = jnp.dot(k_ref[j * tk:(j + 1) * tk, :], qt_ref[:, cols[c]], preferred_element_type=F32)

    def softmax(c, slot, m, l):
        mx = jnp.full((8, ts), NEG_BIG, F32)
        for r in range(0, tk, strip):
            mx = jnp.maximum(mx, jnp.max(s_ref[c, slot, r:r + strip, :].reshape(strip // 8, 8, ts), axis=0))
        m_new = jnp.maximum(m, jnp.max(mx, axis=0, keepdims=True))
        psum = jnp.zeros((8, ts), F32)
        for r in range(0, tk, strip):
            p = jnp.exp2(s_ref[c, slot, r:r + strip, :] - m_new)
            psum = psum + jnp.sum(p.reshape(strip // 8, 8, ts), axis=0)
            p_ref[c, slot, r:r + strip, :] = p.astype(BF16)
        alpha = jnp.exp2(m - m_new)
        return m_new, alpha * l + jnp.sum(psum, axis=0, keepdims=True), alpha

    def values(c, j, slot, alpha):
        acc_ref[c] = alpha * acc_ref[c] + jnp.dot(vt_ref[j], p_ref[c, slot], preferred_element_type=F32)

    stats = []
    for c in range(streams):
        scores(c, 0, 0)
        s_m = jnp.dot(km_ref[...], qt_ref[:, cols[c]], preferred_element_type=F32)
        rows = lax.broadcasted_iota(jnp.int32, s_m.shape, 0)
        s_m = jnp.where(rows < n_meta, s_m, NEG_BIG)
        m = jnp.max(s_m, axis=0, keepdims=True)
        p_m = jnp.exp2(s_m - m)
        l = jnp.sum(p_m, axis=0, keepdims=True)
        acc_ref[c] = jnp.dot(vtm_ref[0], p_m.astype(BF16), preferred_element_type=F32)
        stats.append((m, l, None))
    for j in range(nkb):
        for c in range(streams):
            m, l, alpha = stats[c]
            if j + 1 < nkb:
                scores(c, j + 1, (j + 1) % 2)
            if j > 0:
                values(c, j - 1, (j - 1) % 2, alpha)
            stats[c] = softmax(c, j % 2, m, l)
    for c in range(streams):
        m, l, alpha = stats[c]
        values(c, nkb - 1, (nkb - 1) % 2, alpha)
        o_ref[cols[c], :] = (acc_ref[c] / l).T.astype(o_ref.dtype)


def _cast_blocks(rows, steps):
    d = max(1, min(rows // 16, steps))
    while (rows // 16) % d:
        d -= 1
    return d


def _flash(qt, k, vt, km, vtm, casts, *, n_meta, tq=1024, streams=4):
    heads, kq, M = qt.shape
    nkb, tk = vt.shape[1], vt.shape[3]
    tq = _tile(M, tq)
    tkm = km.shape[1]
    ts = tq // streams
    nq = M // tq
    cast_specs = []
    for c in casts:
        nblk = _cast_blocks(c.shape[0], heads * nq)
        cast_specs.append(pl.BlockSpec((c.shape[0] // nblk, c.shape[1]),
                                       lambda h, i, nblk=nblk: (jnp.minimum(h * nq + i, nblk - 1), 0)))
    res = pl.pallas_call(
        functools.partial(_flash_kernel, nkb=nkb, tk=tk, n_meta=n_meta, streams=streams, n_cast=len(casts)),
        grid=(heads, nq),
        in_specs=[
            pl.BlockSpec((None, kq, tq), lambda h, i: (h, 0, i)),
            pl.BlockSpec((None, M, kq), lambda h, i: (h, 0, 0)),
            pl.BlockSpec((None, nkb, MLA_V, tk), lambda h, i: (h, 0, 0, 0)),
            pl.BlockSpec((None, tkm, kq), lambda h, i: (h, 0, 0)),
            pl.BlockSpec((None, 1, MLA_V, tkm), lambda h, i: (h, 0, 0, 0)),
        ] + cast_specs,
        out_specs=[pl.BlockSpec((tq, MLA_V), lambda h, i: (i, h))] + cast_specs,
        out_shape=[jax.ShapeDtypeStruct((M, heads * MLA_V), BF16)]
        + [jax.ShapeDtypeStruct(c.shape, BF16) for c in casts],
        scratch_shapes=[pltpu.VMEM((streams, 2, tk, ts), F32), pltpu.VMEM((streams, 2, tk, ts), BF16),
                        pltpu.VMEM((streams, MLA_V, ts), F32)],
        compiler_params=_cparams(("arbitrary", "arbitrary")),
        name="mla_flash",
    )(qt, k, vt, km, vtm, *casts)
    return res[0], tuple(res[1:])


def _gla_consts(direction):
    C = GLA_CHUNK
    r = np.arange(C)
    if direction == "fwd":
        cum = (r[None, :] <= r[:, None])
    else:
        cum = (r[None, :] >= r[:, None])
    eq2 = np.zeros((C, C), np.float32)
    ek2 = np.zeros((C, C), np.float32)
    for t in range(C):
        pos = t % 4
        if direction == "fwd":
            if pos == 2:
                eq2[t, t] = 1
            elif pos == 3:
                eq2[t, t] = 1
                eq2[t, t - 1] = 1
            elif pos == 0:
                ek2[t, t + 1] = 1
        else:
            if pos == 1:
                eq2[t, t] = 1
            elif pos == 0:
                eq2[t, t] = 1
                eq2[t, t + 1] = 1
            elif pos == 3:
                ek2[t, t - 1] = 1
    mats = np.concatenate([cum.astype(np.float32), eq2, ek2], axis=0)
    masks = np.zeros((6, C, LANES), np.float32)
    for li, m in enumerate((4, 2, 1)):
        upper = (r % (2 * m)) >= m
        qrows = upper if direction == "fwd" else ~upper
        masks[2 * li] = np.where(qrows, 0.0, NEG_BIG)[:, None]
        masks[2 * li + 1] = np.where(~qrows, 0.0, NEG_BIG)[:, None]
    return jnp.asarray(mats, BF16), jnp.asarray(masks, F32)


def _split3_dot(mat, x):
    x1 = x.astype(BF16)
    r1 = x - x1.astype(F32)
    x2 = r1.astype(BF16)
    x3 = (r1 - x2.astype(F32)).astype(BF16)
    return (jnp.dot(mat, x1, preferred_element_type=F32)
            + jnp.dot(mat, x2, preferred_element_type=F32)
            + jnp.dot(mat, x3, preferred_element_type=F32))


def _log_decay(lr, w_ref, b_ref):
    z = jnp.dot(lr.astype(BF16), w_ref[...], preferred_element_type=F32) + b_ref[...]
    return jax.nn.log_sigmoid(z) * (math.log2(math.e) / GLA_GATE_NORM)


def _gla_chunk(q, k, v, g, st_ref, c_ref, mats_ref, masks_ref, *, heads, direction, want_out):
    C = GLA_CHUNK
    fwd = direction == "fwd"
    e_all = _split3_dot(mats_ref[...], g)
    c_all = e_all[0:C]
    eq2_all = e_all[C:2 * C]
    ek2_all = e_all[2 * C:3 * C]
    c_ref[...] = c_all
    far = C - 1 if fwd else 0
    outs = []
    for h in range(heads):
        ls = slice(h * GLA_DK, (h + 1) * GLA_DK)
        vs = slice(h * GLA_DV, (h + 1) * GLA_DV)
        kh, gh, ch = k[:, ls], g[:, ls], c_all[:, ls]
        qh = q[:, ls] if want_out else None
        vh = v[:, vs]
        tot = c_ref[pl.ds(far, 1), ls]
        st = st_ref[h]
        if want_out:
            xor = lax.broadcasted_iota(jnp.int32, (C, C), 0) ^ lax.broadcasted_iota(jnp.int32, (C, C), 1)
            dg = jnp.sum(qh * kh, axis=-1, keepdims=True)
            att = jnp.where(xor == 0, dg, 0.0)
            qcat, kcat = [], []
            for m in GLA_LEVELS:
                if m >= 8:
                    qp, kp = [], []
                    for r0 in range(0, C, 2 * m):
                        lo = slice(r0, r0 + m)
                        hi = slice(r0 + m, r0 + 2 * m)
                        ref = c_ref[pl.ds(r0 + m - 1 if fwd else r0 + m, 1), ls]
                        qr, kr = (hi, lo) if fwd else (lo, hi)
                        qa = qh[qr] * jnp.exp2(ch[qr] - ref)
                        ka = kh[kr] * jnp.exp2(ref - ch[kr])
                        z = jnp.zeros((m, GLA_DK), F32)
                        qp += [z, qa] if fwd else [qa, z]
                        kp += [ka, z] if fwd else [z, ka]
                    qcat.append(jnp.concatenate(qp, axis=0).astype(BF16))
                    kcat.append(jnp.concatenate(kp, axis=0).astype(BF16))
                else:
                    li = (4, 2, 1).index(m)
                    mq = masks_ref[2 * li]
                    mk = masks_ref[2 * li + 1]
                    if m == 4:
                        ref = jnp.concatenate(
                            [jnp.broadcast_to(c_ref[pl.ds(r0 + 3 if fwd else r0 + 4, 1), ls], (8, GLA_DK))
                             for r0 in range(0, C, 8)], axis=0)
                        eq = ch - ref
                        ek = ref - ch
                    elif m == 2:
                        eq = eq2_all[:, ls]
                        ek = ek2_all[:, ls]
                    else:
                        eq = gh
                        ek = jnp.zeros_like(gh)
                    qcat.append((qh * jnp.exp2(eq + mq)).astype(BF16))
                    kcat.append((kh * jnp.exp2(ek + mk)).astype(BF16))
                pm = lax.dot_general(qcat.pop(), kcat.pop(), (((1,), (1,)), ((), ())),
                                     preferred_element_type=F32)
                att = att + (pm if 2 * m == C else jnp.where(xor < 2 * m, pm, 0.0))
            qs = (qh * jnp.exp2(ch)).astype(BF16)
            o = jnp.dot(att.astype(BF16), vh, preferred_element_type=F32)
            o = o + lax.dot_general(qs, st.astype(BF16), (((1,), (1,)), ((), ())),
                                    preferred_element_type=F32)
            outs.append(o)
        kd = (kh * jnp.exp2(tot - ch)).astype(BF16)
        ut = lax.dot_general(vh, kd, (((0,), (0,)), ((), ())), preferred_element_type=F32)
        st_ref[h] = st * jnp.exp2(tot) + ut
    return outs if want_out else None


def _gla_fwd_kernel(q_ref, k_ref, v_ref, lr_ref, km_ref, vm_ref, lrm_ref, w_ref, b_ref, mats_ref, masks_ref,
                    o_ref, st_ref, c_ref, *, heads, n_meta, cps):
    n = pl.program_id(0)
    common = dict(heads=heads, direction="fwd")

    @pl.when(n == 0)
    def _():
        st_ref[...] = jnp.zeros_like(st_ref)
        g = _log_decay(lrm_ref[...], w_ref, b_ref)
        rows = lax.broadcasted_iota(jnp.int32, g.shape, 0)
        g = jnp.where(rows < n_meta, g, 0.0)
        k = km_ref[...].astype(F32)
        _gla_chunk(None, k, vm_ref[...], g, st_ref, c_ref.at[0], mats_ref, masks_ref, want_out=False, **common)

    @pl.when(n > 0)
    def _():
        for u in range(cps):
            rows = slice(u * GLA_CHUNK, (u + 1) * GLA_CHUNK)
            g = _log_decay(lr_ref[rows, :], w_ref, b_ref)
            q = q_ref[rows, :].astype(F32) * (GLA_DK ** -0.5)
            k = k_ref[rows, :].astype(F32)
            outs = _gla_chunk(q, k, v_ref[rows, :], g, st_ref, c_ref.at[u], mats_ref, masks_ref,
                              want_out=True, **common)
            for h, o in enumerate(outs):
                o_ref[rows, h * GLA_DV:(h + 1) * GLA_DV] = o


def _gla_bwd_kernel(q_ref, k_ref, v_ref, gog_ref, lr_ref, of_ref, w_ref, b_ref, gn_ref, mats_ref, masks_ref,
                    o_ref, st_ref, c_ref, *, heads, cps):
    @pl.when(pl.program_id(0) == 0)
    def _():
        st_ref[...] = jnp.zeros_like(st_ref)

    gn = gn_ref[...]
    for u in reversed(range(cps)):
        rows = slice(u * GLA_CHUNK, (u + 1) * GLA_CHUNK)
        g = _log_decay(lr_ref[rows, :], w_ref, b_ref)
        q = q_ref[rows, :].astype(F32) * (GLA_DK ** -0.5)
        k = k_ref[rows, :].astype(F32)
        outs = _gla_chunk(q, k, v_ref[rows, :], g, st_ref, c_ref.at[u], mats_ref, masks_ref,
                          heads=heads, direction="bwd", want_out=True)
        for h, ob in enumerate(outs):
            vs = slice(h * GLA_DV, (h + 1) * GLA_DV)
            o = _rms(of_ref[rows, vs] + ob, gn)
            o_ref[rows, vs] = (o * jax.nn.silu(gog_ref[rows, vs].astype(F32))).astype(o_ref.dtype)


def _gla(proj_ac, proj_b, km, vm, lrm, wf, bf, wb, bb, gn, *, heads, n_meta, cps=4):
    M = proj_ac.shape[0]
    cps = cps if M % (cps * GLA_CHUNK) == 0 else 1
    C = cps * GLA_CHUNK
    nch = M // C
    qw, vw = heads * GLA_DK, heads * GLA_DV
    const2 = lambda n: (0, 0)
    const3 = lambda n: (0, 0, 0)
    scratch = [pltpu.VMEM((heads, GLA_DV, GLA_DK), F32), pltpu.VMEM((cps, GLA_CHUNK, qw), F32)]
    mats_f, masks_f = _gla_consts("fwd")
    mats_b, masks_b = _gla_consts("bwd")

    xb = lambda n: jnp.maximum(n - 1, 0)
    o_f = pl.pallas_call(
        functools.partial(_gla_fwd_kernel, heads=heads, n_meta=n_meta, cps=cps),
        grid=(nch + 1,),
        in_specs=[
            pl.BlockSpec((C, qw), lambda n: (xb(n), 0)),
            pl.BlockSpec((C, qw), lambda n: (xb(n), 1)),
            pl.BlockSpec((C, vw), lambda n: (xb(n), 1)),
            pl.BlockSpec((C, LANES), lambda n: (xb(n), 0)),
            pl.BlockSpec(km.shape, const2),
            pl.BlockSpec(vm.shape, const2),
            pl.BlockSpec(lrm.shape, const2),
            pl.BlockSpec(wf.shape, const2),
            pl.BlockSpec(bf.shape, const2),
            pl.BlockSpec(mats_f.shape, const2),
            pl.BlockSpec(masks_f.shape, const3),
        ],
        out_specs=pl.BlockSpec((C, vw), lambda n: (xb(n), 0)),
        out_shape=jax.ShapeDtypeStruct((M, vw), F32),
        scratch_shapes=scratch,
        compiler_params=_cparams(("arbitrary",)),
        name="gla_fwd",
    )(proj_ac, proj_ac, proj_ac, proj_b, km, vm, lrm, wf, bf, mats_f, masks_f)

    rb = lambda n: nch - 1 - n
    return pl.pallas_call(
        functools.partial(_gla_bwd_kernel, heads=heads, cps=cps),
        grid=(nch,),
        in_specs=[
            pl.BlockSpec((C, qw), lambda n: (rb(n), 0)),
            pl.BlockSpec((C, qw), lambda n: (rb(n), 1)),
            pl.BlockSpec((C, vw), lambda n: (rb(n), 1)),
            pl.BlockSpec((C, vw), lambda n: (rb(n), 2)),
            pl.BlockSpec((C, LANES), lambda n: (rb(n), 1)),
            pl.BlockSpec((C, vw), lambda n: (rb(n), 0)),
            pl.BlockSpec(wb.shape, const2),
            pl.BlockSpec(bb.shape, const2),
            pl.BlockSpec(gn.shape, const2),
            pl.BlockSpec(mats_b.shape, const2),
            pl.BlockSpec(masks_b.shape, const3),
        ],
        out_specs=pl.BlockSpec((C, vw), lambda n: (rb(n), 0)),
        out_shape=jax.ShapeDtypeStruct((M, vw), BF16),
        scratch_shapes=scratch,
        compiler_params=_cparams(("arbitrary",)),
        name="gla_bwd",
    )(proj_ac, proj_ac, proj_ac, proj_ac, proj_b, o_f, wb, bb, gn, mats_b, masks_b)


def _merge_kernel(og_ref, om_ref, wg_ref, wm_ref, ga_ref, gb_ref, o_ref):
    yg = jnp.dot(og_ref[...], wg_ref[...], preferred_element_type=F32)
    ym = jnp.dot(om_ref[...], wm_ref[...], preferred_element_type=F32)
    ga = jax.nn.sigmoid(ga_ref[...].astype(F32))
    gb = jax.nn.sigmoid(gb_ref[...].astype(F32))
    o_ref[...] = (ga * yg + gb * ym).astype(o_ref.dtype)


def _merge(og, om, wg, wm, gates, *, tm=1024, tn=1024):
    M, D = og.shape[0], wg.shape[1]
    tm, tn = _tile(M, tm), _tile(D, tn)
    gb_blk = D // tn
    return pl.pallas_call(
        _merge_kernel,
        grid=(M // tm, D // tn),
        in_specs=[
            pl.BlockSpec((tm, og.shape[1]), lambda i, j: (i, 0)),
            pl.BlockSpec((tm, om.shape[1]), lambda i, j: (i, 0)),
            pl.BlockSpec((wg.shape[0], tn), lambda i, j: (0, j)),
            pl.BlockSpec((wm.shape[0], tn), lambda i, j: (0, j)),
            pl.BlockSpec((tm, tn), lambda i, j: (i, j)),
            pl.BlockSpec((tm, tn), lambda i, j: (i, j + gb_blk)),
        ],
        out_specs=pl.BlockSpec((tm, tn), lambda i, j: (i, j)),
        out_shape=jax.ShapeDtypeStruct((M, D), BF16),
        compiler_params=_cparams(("arbitrary", "arbitrary")),
        name="merge_out_proj",
    )(og, om, wg, wm, gates, gates)


def _rmm_kernel(a_ref, b_ref, r_ref, o_ref, *acc, nk):
    def part():
        return jnp.dot(a_ref[...], b_ref[...], preferred_element_type=F32)

    if nk == 1:
        o_ref[...] = r_ref[...] + part()
        return
    acc_ref, = acc
    kk = pl.program_id(2)

    @pl.when(kk == 0)
    def _():
        acc_ref[...] = part()

    @pl.when((kk > 0) & (kk < nk - 1))
    def _():
        acc_ref[...] += part()

    @pl.when(kk == nk - 1)
    def _():
        o_ref[...] = r_ref[...] + (acc_ref[...] + part())


def _res_matmul(a, b, res, *, tm=1024, tn=1024, tk=4096, name):
    M, K = a.shape
    N = b.shape[1]
    tm, tn, tk = _tile(M, tm), _tile(N, tn), _tile(K, tk)
    nk = K // tk
    return pl.pallas_call(
        functools.partial(_rmm_kernel, nk=nk),
        grid=(M // tm, N // tn, nk),
        in_specs=[
            pl.BlockSpec((tm, tk), lambda i, j, k: (i, k)),
            pl.BlockSpec((tk, tn), lambda i, j, k: (k, j)),
            pl.BlockSpec((tm, tn), lambda i, j, k: (i, j)),
        ],
        out_specs=pl.BlockSpec((tm, tn), lambda i, j, k: (i, j)),
        out_shape=jax.ShapeDtypeStruct((M, N), F32),
        scratch_shapes=[pltpu.VMEM((tm, tn), F32)] if nk > 1 else [],
        compiler_params=_cparams(("arbitrary", "arbitrary", "arbitrary")),
        name=name,
    )(a, b, res)


def _pad_cols(w, width):
    return jnp.pad(w, ((0, 0), (0, width - w.shape[1])))


def kernel(x, meta_tokens, ln1, w_in, gla_wf, gla_bf, gla_wb, gla_bb, gla_norm, q_norm, w_uq, kv_norm, w_ukv,
           w_gla_out, w_mla_out, w_o, ln2, w_ff1, w_ff2, final_norm):
    assert x.shape[0] == 1 and ln1.shape[0] == 1, "one sequence, one layer"
    S, D = x.shape[1], x.shape[2]
    n_meta = meta_tokens.shape[0]
    rank = gla_wf.shape[1]
    qk_w = gla_wf.shape[2]
    hg = qk_w // GLA_DK
    v_w = hg * GLA_DV
    q_rank, kv_rank = w_uq.shape[1], w_ukv.shape[1]
    hm = w_uq.shape[2] // (MLA_NOPE + MLA_ROPE)
    half = MLA_ROPE // 2
    assert n_meta <= GLA_CHUNK and rank <= LANES and hm % 2 == 0 and S % GLA_CHUNK == 0

    wt = w_in[0].T
    o_gog_end = 2 * qk_w + 2 * v_w
    o_lrb = o_gog_end + rank
    o_cq = o_lrb + rank
    o_ckv = o_cq + q_rank
    o_kr = o_ckv + kv_rank
    o_ga = o_kr + MLA_ROPE
    pad_rows = lambda a: jnp.pad(a, ((0, LANES - a.shape[0]), (0, 0)))
    wt_b = jnp.concatenate([
        pad_rows(wt[o_gog_end:o_lrb]), pad_rows(wt[o_lrb:o_cq]),
        pad_rows(wt[o_kr:o_kr + half]), pad_rows(wt[o_kr + half:o_ga]),
        wt[o_ckv:o_kr], wt[o_cq:o_ckv]], axis=0)
    kr_blk = 1
    ckv_blk = (4 * LANES) // kv_rank
    cq_blk = (4 * LANES + kv_rank) // q_rank
    assert (4 * LANES) % kv_rank == 0 and (4 * LANES + kv_rank) % q_rank == 0

    wq = w_uq[0].reshape(q_rank, hm, MLA_NOPE + MLA_ROPE)
    wq = jnp.concatenate([wq[:, :, :MLA_NOPE].reshape(q_rank, -1), wq[:, :, MLA_NOPE:].reshape(q_rank, -1)],
                         axis=1).astype(BF16)
    wkv = w_ukv[0].astype(BF16)
    wf = jnp.pad(gla_wf[0], ((0, LANES - rank), (0, 0))).astype(BF16)
    wb = jnp.pad(gla_wb[0], ((0, LANES - rank), (0, 0))).astype(BF16)

    inv_freq = ROPE_THETA ** (-jnp.arange(0, MLA_ROPE, 2, dtype=F32) / MLA_ROPE)
    ang = jnp.arange(n_meta + S, dtype=F32)[:, None] * inv_freq[None, :]
    cos, sin = jnp.cos(ang), jnp.sin(ang)
    cos_x, sin_x = cos[n_meta:], sin[n_meta:]
    cos_xp, sin_xp = _pad_cols(cos_x, LANES), _pad_cols(sin_x, LANES)
    mrows = GLA_CHUNK
    padm = lambda a: jnp.pad(a, ((0, mrows - n_meta), (0, LANES - a.shape[1])))
    cos_mp, sin_mp = padm(cos[:n_meta]), padm(sin[:n_meta])

    xs = x[0]
    xm = jnp.pad(meta_tokens.astype(F32), ((0, mrows - n_meta), (0, 0)))
    g1 = ln1[0][None, :]

    xn = _rms_rows(xs, g1, out_dtype=BF16, name="ln1_x")
    xmn = _rms_rows(xm, g1, out_dtype=BF16, name="ln1_meta")
    proj_a = _ws_matmul(xn, wt, col0=0, n_out=o_gog_end, out_dtype=BF16, tm=1024, tn=512, name="in_proj_a")
    gates = _ws_matmul(xn, wt, col0=o_ga, n_out=2 * D, out_dtype=BF16, tm=1024, tn=512, name="in_proj_gates")
    nb = wt_b.shape[0]
    proj_b = _ws_matmul(xn, wt_b, col0=0, n_out=nb, out_dtype=F32, tm=1024, tn=512, name="in_proj_b")
    proj_b_m = _ws_matmul(xmn, wt_b, col0=0, n_out=nb, out_dtype=F32, tm=mrows, tn=512, name="in_proj_b_meta")
    kv_m = _ws_matmul(xmn, wt, col0=qk_w, n_out=qk_w + v_w, out_dtype=BF16, tm=mrows, tn=512,
                      name="in_proj_kv_meta")

    og = _gla(proj_a, proj_b, kv_m[:, :qk_w], kv_m[:, qk_w:], proj_b_m[:, :LANES],
              wf, gla_bf, wb, gla_bb, gla_norm, heads=hg, n_meta=n_meta)

    qt = _q_up(proj_b, cq_blk, q_norm, wq, cos_x.T, sin_x.T, heads=hm)
    gkv = kv_norm
    k_x, vt_x = _kv_up(proj_b, gkv, wkv, cos_xp, sin_xp, heads=hm, ckv_blk=ckv_blk, kr_blk=kr_blk,
                       tm=_tile(S, 512))
    k_m, vt_m = _kv_up(proj_b_m, gkv, wkv, cos_mp, sin_mp, heads=hm, ckv_blk=ckv_blk, kr_blk=kr_blk, tm=mrows)
    om, (wgo, wmo, wo, w1, w2) = _flash(qt, k_x, vt_x, k_m, vt_m,
                                        (w_gla_out[0], w_mla_out[0], w_o[0], w_ff1[0], w_ff2[0]), n_meta=n_meta)

    merged = _merge(og, om, wgo, wmo, gates)
    h1 = _res_matmul(merged, wo, xs, name="o_proj")
    hf = _norm_matmul(h1, ln2[0][None, :], w1, out_dtype=BF16, act="relu2", name="ffn_up")
    h2 = _res_matmul(hf, w2, h1, tk=2048, name="ffn_down")
    return _rms_rows(h2, final_norm[None, :], out_dtype=F32, name="final_norm")[None]
```

```python
import functools
import math

import jax
import jax.numpy as jnp
import numpy as np
from jax import lax
from jax.experimental import pallas as pl
from jax.experimental.pallas import tpu as pltpu

F32 = jnp.float32
BF16 = jnp.bfloat16

GLA_DK = 128
GLA_DV = 256
GLA_GATE_NORM = 16.0
MLA_NOPE = 128
MLA_ROPE = 64
MLA_V = 128
ROPE_THETA = 10000.0
EPS = 1e-6

LANES = 128
GLA_CHUNK = 128
GLA_LEVELS = (64, 32, 16, 8, 4, 2, 1)
NEG_BIG = -1e30
VMEM_LIMIT = 56 * 1024 * 1024


def _cparams(sem):
    return pltpu.CompilerParams(dimension_semantics=sem, vmem_limit_bytes=VMEM_LIMIT)


def _tile(n, pref):
    t = min(n, pref)
    while n % t:
        t //= 2
    return t


def _rms(a, g, eps=EPS):
    ms = jnp.mean(a * a, axis=-1, keepdims=True)
    return a * lax.rsqrt(ms + eps) * g


def _nmm_kernel(a_ref, g_ref, b_ref, o_ref, an_ref, *, act):
    def project(an):
        y = jnp.dot(an, b_ref[...], preferred_element_type=F32)
        if act == "relu2":
            y = jnp.square(jnp.maximum(y, 0.0))
        o_ref[...] = y.astype(o_ref.dtype)

    @pl.when(pl.program_id(1) == 0)
    def _():
        an = _rms(a_ref[...], g_ref[...]).astype(BF16)
        an_ref[...] = an
        project(an)

    @pl.when(pl.program_id(1) > 0)
    def _():
        project(an_ref[...])


def _norm_matmul(a, g, b, *, out_dtype, tm=512, tn=1024, act=None, name):
    M, K = a.shape
    N = b.shape[1]
    tm, tn = _tile(M, tm), _tile(N, tn)
    return pl.pallas_call(
        functools.partial(_nmm_kernel, act=act),
        grid=(M // tm, N // tn),
        in_specs=[
            pl.BlockSpec((tm, K), lambda i, j: (i, 0)),
            pl.BlockSpec((1, K), lambda i, j: (0, 0)),
            pl.BlockSpec((K, tn), lambda i, j: (0, j)),
        ],
        out_specs=pl.BlockSpec((tm, tn), lambda i, j: (i, j)),
        out_shape=jax.ShapeDtypeStruct((M, N), out_dtype),
        scratch_shapes=[pltpu.VMEM((tm, K), BF16)],
        compiler_params=_cparams(("arbitrary", "arbitrary")),
        name=name,
    )(a, g, b)


def _nmm_emit_kernel(a_ref, g_ref, b_ref, an_ref, o_ref):
    def project(an):
        o_ref[...] = jnp.dot(an, b_ref[...], preferred_element_type=F32).astype(o_ref.dtype)

    @pl.when(pl.program_id(1) == 0)
    def _():
        an = _rms(a_ref[...], g_ref[...]).astype(BF16)
        an_ref[...] = an
        project(an)

    @pl.when(pl.program_id(1) > 0)
    def _():
        project(an_ref[...])


def _norm_matmul_emit(a, g, b, *, out_dtype, tm=512, tn=1024, name):
    M, K = a.shape
    N = b.shape[1]
    tm, tn = _tile(M, tm), _tile(N, tn)
    return pl.pallas_call(
        _nmm_emit_kernel,
        grid=(M // tm, N // tn),
        in_specs=[
            pl.BlockSpec((tm, K), lambda i, j: (i, 0)),
            pl.BlockSpec((1, K), lambda i, j: (0, 0)),
            pl.BlockSpec((K, tn), lambda i, j: (0, j)),
        ],
        out_specs=[pl.BlockSpec((tm, K), lambda i, j: (i, 0)), pl.BlockSpec((tm, tn), lambda i, j: (i, j))],
        out_shape=[jax.ShapeDtypeStruct((M, K), BF16), jax.ShapeDtypeStruct((M, N), out_dtype)],
        compiler_params=_cparams(("arbitrary", "arbitrary")),
        name=name,
    )(a, g, b)


def _rms_rows_kernel(x_ref, g_ref, o_ref):
    o_ref[...] = _rms(x_ref[...], g_ref[...]).astype(o_ref.dtype)


def _rms_rows(x, g, *, out_dtype, tm=256, name):
    M, D = x.shape
    tm = _tile(M, tm)
    return pl.pallas_call(
        _rms_rows_kernel,
        grid=(M // tm,),
        in_specs=[pl.BlockSpec((tm, D), lambda i: (i, 0)), pl.BlockSpec((1, D), lambda i: (0, 0))],
        out_specs=pl.BlockSpec((tm, D), lambda i: (i, 0)),
        out_shape=jax.ShapeDtypeStruct((M, D), out_dtype),
        compiler_params=_cparams(("arbitrary",)),
        name=name,
    )(x, g)


def _ws_kernel(*refs, shift):
    a_ref, w_ref = refs[0], refs[1]
    w2_ref = refs[2] if shift is not None else None
    o_ref, wb_ref = refs[-2], refs[-1]
    K, tn = wb_ref.shape

    def project():
        o_ref[...] = jnp.dot(a_ref[...], wb_ref[...], preferred_element_type=F32).astype(o_ref.dtype)

    @pl.when(pl.program_id(1) == 0)
    def _():
        kc = _tile(K, 512)
        for r in range(0, K, kc):
            if shift is None:
                blk = w_ref[:, r:r + kc]
            else:
                blk = jnp.concatenate([w_ref[shift:, r:r + kc], w2_ref[:shift, r:r + kc]], axis=0)
            wb_ref[r:r + kc, :] = blk.T.astype(BF16)
        project()

    pl.when(pl.program_id(1) > 0)(project)


def _ws_matmul(a, wt, *, col0, n_out, out_dtype, tm, tn, name):
    M, K = a.shape
    shift = col0 % LANES
    base = col0 - shift
    tm, tn = _tile(M, tm), _tile(math.gcd(n_out, base), tn)
    nj, ni = n_out // tn, M // tm
    in_specs = [pl.BlockSpec((tm, K), lambda j, i: (i, 0)),
                pl.BlockSpec((tn, K), lambda j, i: (j + base // tn, 0))]
    operands = [a, wt]
    if shift:
        in_specs.append(pl.BlockSpec((LANES, K), lambda j, i: ((j + 1) * (tn // LANES) + base // LANES, 0)))
        operands.append(wt)
    return pl.pallas_call(
        functools.partial(_ws_kernel, shift=shift if shift else None),
        grid=(nj, ni),
        in_specs=in_specs,
        out_specs=pl.BlockSpec((tm, tn), lambda j, i: (i, j)),
        out_shape=jax.ShapeDtypeStruct((M, n_out), out_dtype),
        scratch_shapes=[pltpu.VMEM((K, tn), BF16)],
        compiler_params=_cparams(("arbitrary", "arbitrary")),
        name=name,
    )(*operands)


def _q_up_kernel(cq_ref, g_ref, w_ref, cos_ref, sin_ref, qt_ref, *, heads, scale):
    cn = _rms(cq_ref[...], g_ref[...]).astype(BF16)
    y = jnp.dot(cn, w_ref[...], preferred_element_type=F32) * scale
    c = cos_ref[...]
    s = sin_ref[...]
    half = MLA_ROPE // 2
    for h in range(heads):
        qt_ref[h, 0:MLA_NOPE, :] = y[:, h * MLA_NOPE:(h + 1) * MLA_NOPE].T.astype(BF16)
    rope0 = heads * MLA_NOPE
    for p in range(heads // 2):
        blk = y[:, rope0 + p * LANES: rope0 + (p + 1) * LANES].T
        for u in range(2):
            x1 = blk[u * MLA_ROPE: u * MLA_ROPE + half]
            x2 = blk[u * MLA_ROPE + half: (u + 1) * MLA_ROPE]
            h = 2 * p + u
            qt_ref[h, MLA_NOPE:MLA_NOPE + half, :] = (x1 * c - x2 * s).astype(BF16)
            qt_ref[h, MLA_NOPE + half:MLA_NOPE + MLA_ROPE, :] = (x1 * s + x2 * c).astype(BF16)
    pad = qt_ref.shape[1] - MLA_NOPE - MLA_ROPE
    for h in range(heads):
        qt_ref[h, MLA_NOPE + MLA_ROPE:, :] = jnp.zeros((pad, qt_ref.shape[2]), BF16)


def _q_up(proj_b, cq_blk, g, w, cos_t, sin_t, *, heads, tm=512):
    M = proj_b.shape[0]
    R = g.shape[1]
    tm = _tile(M, tm)
    kq = 2 * LANES
    return pl.pallas_call(
        functools.partial(_q_up_kernel, heads=heads, scale=float((MLA_NOPE + MLA_ROPE) ** -0.5 * math.log2(math.e))),
        grid=(M // tm,),
        in_specs=[
            pl.BlockSpec((tm, R), lambda i: (i, cq_blk)),
            pl.BlockSpec((1, R), lambda i: (0, 0)),
            pl.BlockSpec(w.shape, lambda i: (0, 0)),
            pl.BlockSpec((MLA_ROPE // 2, tm), lambda i: (0, i)),
            pl.BlockSpec((MLA_ROPE // 2, tm), lambda i: (0, i)),
        ],
        out_specs=pl.BlockSpec((heads, kq, tm), lambda i: (0, 0, i)),
        out_shape=jax.ShapeDtypeStruct((heads, kq, M), BF16),
        compiler_params=_cparams(("arbitrary",)),
        name="mla_q_up",
    )(proj_b, g, w, cos_t, sin_t)


def _kv_up_kernel(ckv_ref, kr_ref, g_ref, w_ref, cos_ref, sin_ref, k_ref, vt_ref, *, heads):
    cn = _rms(ckv_ref[...], g_ref[...]).astype(BF16)
    y = jnp.dot(cn, w_ref[...], preferred_element_type=F32)
    x1 = kr_ref[:, 0:LANES]
    x2 = kr_ref[:, LANES:2 * LANES]
    c = cos_ref[...]
    s = sin_ref[...]
    kpe = (x1 * c - x2 * s) + pltpu.roll(x1 * s + x2 * c, MLA_ROPE // 2, axis=1)
    kpe = kpe.astype(BF16)
    hw = MLA_NOPE + MLA_V
    for h in range(heads):
        k_ref[h, :, 0:MLA_NOPE] = y[:, h * hw: h * hw + MLA_NOPE].astype(BF16)
        k_ref[h, :, MLA_NOPE:] = kpe
        vt_ref[h, 0] = y[:, h * hw + MLA_NOPE:(h + 1) * hw].T.astype(BF16)


def _kv_up(proj_b, g, w, cos_p, sin_p, *, heads, ckv_blk, kr_blk, tm):
    M = proj_b.shape[0]
    R = g.shape[1]
    return pl.pallas_call(
        functools.partial(_kv_up_kernel, heads=heads),
        grid=(M // tm,),
        in_specs=[
            pl.BlockSpec((tm, R), lambda i: (i, ckv_blk)),
            pl.BlockSpec((tm, 2 * LANES), lambda i: (i, kr_blk)),
            pl.BlockSpec((1, R), lambda i: (0, 0)),
            pl.BlockSpec(w.shape, lambda i: (0, 0)),
            pl.BlockSpec((tm, LANES), lambda i: (i, 0)),
            pl.BlockSpec((tm, LANES), lambda i: (i, 0)),
        ],
        out_specs=[
            pl.BlockSpec((heads, tm, 2 * LANES), lambda i: (0, i, 0)),
            pl.BlockSpec((heads, 1, MLA_V, tm), lambda i: (0, i, 0, 0)),
        ],
        out_shape=[
            jax.ShapeDtypeStruct((heads, M, 2 * LANES), BF16),
            jax.ShapeDtypeStruct((heads, M // tm, MLA_V, tm), BF16),
        ],
        compiler_params=_cparams(("arbitrary",)),
        name="mla_kv_up",
    )(proj_b, proj_b, g, w, cos_p, sin_p)


def _flash_kernel(*refs, nkb, tk, n_meta, streams, n_cast):
    qt_ref, k_ref, vt_ref, km_ref, vtm_ref = refs[:5]
    cast_in = refs[5:5 + n_cast]
    o_ref = refs[5 + n_cast]
    cast_out = refs[6 + n_cast:6 + 2 * n_cast]
    s_ref, p_ref, acc_ref = refs[6 + 2 * n_cast:]
    for ci, co in zip(cast_in, cast_out):
        co[...] = ci[...].astype(BF16)
    ts = qt_ref.shape[1] // streams
    strip = min(tk, 64)
    cols = [slice(c * ts, (c + 1) * ts) for c in range(streams)]

    def scores(c, j, slot):
        s_ref[c, slot] = jnp.dot(k_ref[j * tk:(j + 1) * tk, :], qt_ref[:, cols[c]], preferred_element_type=F32)

    def softmax(c, slot, m, l):
        mx = jnp.full((8, ts), NEG_BIG, F32)
        for r in range(0, tk, strip):
            mx = jnp.maximum(mx, jnp.max(s_ref[c, slot, r:r + strip, :].reshape(strip // 8, 8, ts), axis=0))
        m_new = jnp.maximum(m, jnp.max(mx, axis=0, keepdims=True))
        psum = jnp.zeros((8, ts), F32)
        for r in range(0, tk, strip):
            p = jnp.exp2(s_ref[c, slot, r:r + strip, :] - m_new)
            psum = psum + jnp.sum(p.reshape(strip // 8, 8, ts), axis=0)
            p_ref[c, slot, r:r + strip, :] = p.astype(BF16)
        alpha = jnp.exp2(m - m_new)
        return m_new, alpha * l + jnp.sum(psum, axis=0, keepdims=True), alpha

    def values(c, j, slot, alpha):
        acc_ref[c] = alpha * acc_ref[c] + jnp.dot(vt_ref[j], p_ref[c, slot], preferred_element_type=F32)

    stats = []
    for c in range(streams):
        scores(c, 0, 0)
        s_m = jnp.dot(km_ref[...], qt_ref[:, cols[c]], preferred_element_type=F32)
        rows = lax.broadcasted_iota(jnp.int32, s_m.shape, 0)
        s_m = jnp.where(rows < n_meta, s_m, NEG_BIG)
        m = jnp.max(s_m, axis=0, keepdims=True)
        p_m = jnp.exp2(s_m - m)
        l = jnp.sum(p_m, axis=0, keepdims=True)
        acc_ref[c] = jnp.dot(vtm_ref[0], p_m.astype(BF16), preferred_element_type=F32)
        stats.append((m, l, None))
    for j in range(nkb):
        for c in range(streams):
            m, l, alpha = stats[c]
            if j + 1 < nkb:
                scores(c, j + 1, (j + 1) % 2)
            if j > 0:
                values(c, j - 1, (j - 1) % 2, alpha)
            stats[c] = softmax(c, j % 2, m, l)
    for c in range(streams):
        m, l, alpha = stats[c]
        values(c, nkb - 1, (nkb - 1) % 2, alpha)
        o_ref[cols[c], :] = (acc_ref[c] / l).T.astype(o_ref.dtype)


def _cast_blocks(rows, steps):
    d = max(1, min(rows // 16, steps))
    while (rows // 16) % d:
        d -= 1
    return d


def _flash(qt, k, vt, km, vtm, casts, *, n_meta, tq=1024, streams=4):
    heads, kq, M = qt.shape
    nkb, tk = vt.shape[1], vt.shape[3]
    tq = _tile(M, tq)
    tkm = km.shape[1]
    ts = tq // streams
    nq = M // tq
    cast_specs = []
    for c in casts:
        nblk = _cast_blocks(c.shape[0], heads * nq)
        cast_specs.append(pl.BlockSpec((c.shape[0] // nblk, c.shape[1]),
                                       lambda h, i, nblk=nblk: (jnp.minimum(h * nq + i, nblk - 1), 0)))
    res = pl.pallas_call(
        functools.partial(_flash_kernel, nkb=nkb, tk=tk, n_meta=n_meta, streams=streams, n_cast=len(casts)),
        grid=(heads, nq),
        in_specs=[
            pl.BlockSpec((None, kq, tq), lambda h, i: (h, 0, i)),
            pl.BlockSpec((None, M, kq), lambda h, i: (h, 0, 0)),
            pl.BlockSpec((None, nkb, MLA_V, tk), lambda h, i: (h, 0, 0, 0)),
            pl.BlockSpec((None, tkm, kq), lambda h, i: (h, 0, 0)),
            pl.BlockSpec((None, 1, MLA_V, tkm), lambda h, i: (h, 0, 0, 0)),
        ] + cast_specs,
        out_specs=[pl.BlockSpec((tq, MLA_V), lambda h, i: (i, h))] + cast_specs,
        out_shape=[jax.ShapeDtypeStruct((M, heads * MLA_V), BF16)]
        + [jax.ShapeDtypeStruct(c.shape, BF16) for c in casts],
        scratch_shapes=[pltpu.VMEM((streams, 2, tk, ts), F32), pltpu.VMEM((streams, 2, tk, ts), BF16),
                        pltpu.VMEM((streams, MLA_V, ts), F32)],
        compiler_params=_cparams(("arbitrary", "arbitrary")),
        name="mla_flash",
    )(qt, k, vt, km, vtm, *casts)
    return res[0], tuple(res[1:])


def _gla_consts(direction):
    C = GLA_CHUNK
    r = np.arange(C)
    if direction == "fwd":
        cum = (r[None, :] <= r[:, None])
    else:
        cum = (r[None, :] >= r[:, None])
    eq2 = np.zeros((C, C), np.float32)
    ek2 = np.zeros((C, C), np.float32)
    for t in range(C):
        pos = t % 4
        if direction == "fwd":
            if pos == 2:
                eq2[t, t] = 1
            elif pos == 3:
                eq2[t, t] = 1
                eq2[t, t - 1] = 1
            elif pos == 0:
                ek2[t, t + 1] = 1
        else:
            if pos == 1:
                eq2[t, t] = 1
            elif pos == 0:
                eq2[t, t] = 1
                eq2[t, t + 1] = 1
            elif pos == 3:
                ek2[t, t - 1] = 1
    mats = np.concatenate([cum.astype(np.float32), eq2, ek2], axis=0)
    masks = np.zeros((6, C, LANES), np.float32)
    for li, m in enumerate((4, 2, 1)):
        upper = (r % (2 * m)) >= m
        qrows = upper if direction == "fwd" else ~upper
        masks[2 * li] = np.where(qrows, 0.0, NEG_BIG)[:, None]
        masks[2 * li + 1] = np.where(~qrows, 0.0, NEG_BIG)[:, None]
    return jnp.asarray(mats, BF16), jnp.asarray(masks, F32)


def _split3_dot(mat, x):
    x1 = x.astype(BF16)
    r1 = x - x1.astype(F32)
    x2 = r1.astype(BF16)
    x3 = (r1 - x2.astype(F32)).astype(BF16)
    return (jnp.dot(mat, x1, preferred_element_type=F32)
            + jnp.dot(mat, x2, preferred_element_type=F32)
            + jnp.dot(mat, x3, preferred_element_type=F32))


def _log_decay(lr, w_ref, b_ref):
    z = jnp.dot(lr.astype(BF16), w_ref[...], preferred_element_type=F32) + b_ref[...]
    return jax.nn.log_sigmoid(z) * (math.log2(math.e) / GLA_GATE_NORM)


def _gla_chunk(q, k, v, g, st_ref, c_ref, mats_ref, masks_ref, *, heads, direction, want_out):
    C = GLA_CHUNK
    fwd = direction == "fwd"
    e_all = _split3_dot(mats_ref[...], g)
    c_all = e_all[0:C]
    eq2_all = e_all[C:2 * C]
    ek2_all = e_all[2 * C:3 * C]
    c_ref[...] = c_all
    far = C - 1 if fwd else 0
    outs = []
    for h in range(heads):
        ls = slice(h * GLA_DK, (h + 1) * GLA_DK)
        vs = slice(h * GLA_DV, (h + 1) * GLA_DV)
        kh, gh, ch = k[:, ls], g[:, ls], c_all[:, ls]
        qh = q[:, ls] if want_out else None
        vh = v[:, vs]
        tot = c_ref[pl.ds(far, 1), ls]
        st = st_ref[h]
        if want_out:
            xor = lax.broadcasted_iota(jnp.int32, (C, C), 0) ^ lax.broadcasted_iota(jnp.int32, (C, C), 1)
            dg = jnp.sum(qh * kh, axis=-1, keepdims=True)
            att = jnp.where(xor == 0, dg, 0.0)
            qcat, kcat = [], []
            for m in GLA_LEVELS:
                if m >= 8:
                    qp, kp = [], []
                    for r0 in range(0, C, 2 * m):
                        lo = slice(r0, r0 + m)
                        hi = slice(r0 + m, r0 + 2 * m)
                        ref = c_ref[pl.ds(r0 + m - 1 if fwd else r0 + m, 1), ls]
                        qr, kr = (hi, lo) if fwd else (lo, hi)
                        qa = qh[qr] * jnp.exp2(ch[qr] - ref)
                        ka = kh[kr] * jnp.exp2(ref - ch[kr])
                        z = jnp.zeros((m, GLA_DK), F32)
                        qp += [z, qa] if fwd else [qa, z]
                        kp += [ka, z] if fwd else [z, ka]
                    qcat.append(jnp.concatenate(qp, axis=0).astype(BF16))
                    kcat.append(jnp.concatenate(kp, axis=0).astype(BF16))
                else:
                    li = (4, 2, 1).index(m)
                    mq = masks_ref[2 * li]
                    mk = masks_ref[2 * li + 1]
                    if m == 4:
                        ref = jnp.concatenate(
                            [jnp.broadcast_to(c_ref[pl.ds(r0 + 3 if fwd else r0 + 4, 1), ls], (8, GLA_DK))
                             for r0 in range(0, C, 8)], axis=0)
                        eq = ch - ref
                        ek = ref - ch
                    elif m == 2:
                        eq = eq2_all[:, ls]
                        ek = ek2_all[:, ls]
                    else:
                        eq = gh
                        ek = jnp.zeros_like(gh)
                    qcat.append((qh * jnp.exp2(eq + mq)).astype(BF16))
                    kcat.append((kh * jnp.exp2(ek + mk)).astype(BF16))
                pm = lax.dot_general(qcat.pop(), kcat.pop(), (((1,), (1,)), ((), ())),
                                     preferred_element_type=F32)
                att = att + (pm if 2 * m == C else jnp.where(xor < 2 * m, pm, 0.0))
            qs = (qh * jnp.exp2(ch)).astype(BF16)
            o = jnp.dot(att.astype(BF16), vh, preferred_element_type=F32)
            o = o + lax.dot_general(qs, st.astype(BF16), (((1,), (1,)), ((), ())),
                                    preferred_element_type=F32)
            outs.append(o)
        kd = (kh * jnp.exp2(tot - ch)).astype(BF16)
        ut = lax.dot_general(vh, kd, (((0,), (0,)), ((), ())), preferred_element_type=F32)
        st_ref[h] = st * jnp.exp2(tot) + ut
    return outs if want_out else None


def _gla_fwd_kernel(q_ref, k_ref, v_ref, lr_ref, km_ref, vm_ref, lrm_ref, w_ref, b_ref, mats_ref, masks_ref,
                    o_ref, st_ref, c_ref, *, heads, n_meta, cps):
    n = pl.program_id(0)
    common = dict(heads=heads, direction="fwd")

    @pl.when(n == 0)
    def _():
        st_ref[...] = jnp.zeros_like(st_ref)
        g = _log_decay(lrm_ref[...], w_ref, b_ref)
        rows = lax.broadcasted_iota(jnp.int32, g.shape, 0)
        g = jnp.where(rows < n_meta, g, 0.0)
        k = km_ref[...].astype(F32)
        _gla_chunk(None, k, vm_ref[...], g, st_ref, c_ref.at[0], mats_ref, masks_ref, want_out=False, **common)

    @pl.when(n > 0)
    def _():
        for u in range(cps):
            rows = slice(u * GLA_CHUNK, (u + 1) * GLA_CHUNK)
            g = _log_decay(lr_ref[rows, :], w_ref, b_ref)
            q = q_ref[rows, :].astype(F32) * (GLA_DK ** -0.5)
            k = k_ref[rows, :].astype(F32)
            outs = _gla_chunk(q, k, v_ref[rows, :], g, st_ref, c_ref.at[u], mats_ref, masks_ref,
                              want_out=True, **common)
            for h, o in enumerate(outs):
                o_ref[rows, h * GLA_DV:(h + 1) * GLA_DV] = o


def _gla_bwd_kernel(q_ref, k_ref, v_ref, gog_ref, lr_ref, of_ref, w_ref, b_ref, gn_ref, mats_ref, masks_ref,
                    o_ref, st_ref, c_ref, *, heads, cps):
    @pl.when(pl.program_id(0) == 0)
    def _():
        st_ref[...] = jnp.zeros_like(st_ref)

    gn = gn_ref[...]
    for u in reversed(range(cps)):
        rows = slice(u * GLA_CHUNK, (u + 1) * GLA_CHUNK)
        g = _log_decay(lr_ref[rows, :], w_ref, b_ref)
        q = q_ref[rows, :].astype(F32) * (GLA_DK ** -0.5)
        k = k_ref[rows, :].astype(F32)
        outs = _gla_chunk(q, k, v_ref[rows, :], g, st_ref, c_ref.at[u], mats_ref, masks_ref,
                          heads=heads, direction="bwd", want_out=True)
        for h, ob in enumerate(outs):
            vs = slice(h * GLA_DV, (h + 1) * GLA_DV)
            o = _rms(of_ref[rows, vs] + ob, gn)
            o_ref[rows, vs] = (o * jax.nn.silu(gog_ref[rows, vs].astype(F32))).astype(o_ref.dtype)


def _gla(proj_ac, proj_b, km, vm, lrm, wf, bf, wb, bb, gn, *, heads, n_meta, cps=4):
    M = proj_ac.shape[0]
    cps = cps if M % (cps * GLA_CHUNK) == 0 else 1
    C = cps * GLA_CHUNK
    nch = M // C
    qw, vw = heads * GLA_DK, heads * GLA_DV
    const2 = lambda n: (0, 0)
    const3 = lambda n: (0, 0, 0)
    scratch = [pltpu.VMEM((heads, GLA_DV, GLA_DK), F32), pltpu.VMEM((cps, GLA_CHUNK, qw), F32)]
    mats_f, masks_f = _gla_consts("fwd")
    mats_b, masks_b = _gla_consts("bwd")

    xb = lambda n: jnp.maximum(n - 1, 0)
    o_f = pl.pallas_call(
        functools.partial(_gla_fwd_kernel, heads=heads, n_meta=n_meta, cps=cps),
        grid=(nch + 1,),
        in_specs=[
            pl.BlockSpec((C, qw), lambda n: (xb(n), 0)),
            pl.BlockSpec((C, qw), lambda n: (xb(n), 1)),
            pl.BlockSpec((C, vw), lambda n: (xb(n), 1)),
            pl.BlockSpec((C, LANES), lambda n: (xb(n), 0)),
            pl.BlockSpec(km.shape, const2),
            pl.BlockSpec(vm.shape, const2),
            pl.BlockSpec(lrm.shape, const2),
            pl.BlockSpec(wf.shape, const2),
            pl.BlockSpec(bf.shape, const2),
            pl.BlockSpec(mats_f.shape, const2),
            pl.BlockSpec(masks_f.shape, const3),
        ],
        out_specs=pl.BlockSpec((C, vw), lambda n: (xb(n), 0)),
        out_shape=jax.ShapeDtypeStruct((M, vw), F32),
        scratch_shapes=scratch,
        compiler_params=_cparams(("arbitrary",)),
        name="gla_fwd",
    )(proj_ac, proj_ac, proj_ac, proj_b, km, vm, lrm, wf, bf, mats_f, masks_f)

    rb = lambda n: nch - 1 - n
    return pl.pallas_call(
        functools.partial(_gla_bwd_kernel, heads=heads, cps=cps),
        grid=(nch,),
        in_specs=[
            pl.BlockSpec((C, qw), lambda n: (rb(n), 0)),
            pl.BlockSpec((C, qw), lambda n: (rb(n), 1)),
            pl.BlockSpec((C, vw), lambda n: (rb(n), 1)),
            pl.BlockSpec((C, vw), lambda n: (rb(n), 2)),
            pl.BlockSpec((C, LANES), lambda n: (rb(n), 1)),
            pl.BlockSpec((C, vw), lambda n: (rb(n), 0)),
            pl.BlockSpec(wb.shape, const2),
            pl.BlockSpec(bb.shape, const2),
            pl.BlockSpec(gn.shape, const2),
            pl.BlockSpec(mats_b.shape, const2),
            pl.BlockSpec(masks_b.shape, const3),
        ],
        out_specs=pl.BlockSpec((C, vw), lambda n: (rb(n), 0)),
        out_shape=jax.ShapeDtypeStruct((M, vw), BF16),
        scratch_shapes=scratch,
        compiler_params=_cparams(("arbitrary",)),
        name="gla_bwd",
    )(proj_ac, proj_ac, proj_ac, proj_ac, proj_b, o_f, wb, bb, gn, mats_b, masks_b)


def _merge_kernel(og_ref, om_ref, wg_ref, wm_ref, ga_ref, gb_ref, o_ref):
    yg = jnp.dot(og_ref[...], wg_ref[...], preferred_element_type=F32)
    ym = jnp.dot(om_ref[...], wm_ref[...], preferred_element_type=F32)
    ga = jax.nn.sigmoid(ga_ref[...].astype(F32))
    gb = jax.nn.sigmoid(gb_ref[...].astype(F32))
    o_ref[...] = (ga * yg + gb * ym).astype(o_ref.dtype)


def _merge(og, om, wg, wm, gates, *, tm=1024, tn=1024):
    M, D = og.shape[0], wg.shape[1]
    tm, tn = _tile(M, tm), _tile(D, tn)
    gb_blk = D // tn
    return pl.pallas_call(
        _merge_kernel,
        grid=(M // tm, D // tn),
        in_specs=[
            pl.BlockSpec((tm, og.shape[1]), lambda i, j: (i, 0)),
            pl.BlockSpec((tm, om.shape[1]), lambda i, j: (i, 0)),
            pl.BlockSpec((wg.shape[0], tn), lambda i, j: (0, j)),
            pl.BlockSpec((wm.shape[0], tn), lambda i, j: (0, j)),
            pl.BlockSpec((tm, tn), lambda i, j: (i, j)),
            pl.BlockSpec((tm, tn), lambda i, j: (i, j + gb_blk)),
        ],
        out_specs=pl.BlockSpec((tm, tn), lambda i, j: (i, j)),
        out_shape=jax.ShapeDtypeStruct((M, D), BF16),
        compiler_params=_cparams(("arbitrary", "arbitrary")),
        name="merge_out_proj",
    )(og, om, wg, wm, gates, gates)


def _rmm_kernel(a_ref, b_ref, r_ref, o_ref, *acc, nk):
    def part():
        return jnp.dot(a_ref[...], b_ref[...], preferred_element_type=F32)

    if nk == 1:
        o_ref[...] = r_ref[...] + part()
        return
    acc_ref, = acc
    kk = pl.program_id(2)

    @pl.when(kk == 0)
    def _():
        acc_ref[...] = part()

    @pl.when((kk > 0) & (kk < nk - 1))
    def _():
        acc_ref[...] += part()

    @pl.when(kk == nk - 1)
    def _():
        o_ref[...] = r_ref[...] + (acc_ref[...] + part())


def _res_matmul(a, b, res, *, tm=1024, tn=1024, tk=4096, name):
    M, K = a.shape
    N = b.shape[1]
    tm, tn, tk = _tile(M, tm), _tile(N, tn), _tile(K, tk)
    nk = K // tk
    return pl.pallas_call(
        functools.partial(_rmm_kernel, nk=nk),
        grid=(M // tm, N // tn, nk),
        in_specs=[
            pl.BlockSpec((tm, tk), lambda i, j, k: (i, k)),
            pl.BlockSpec((tk, tn), lambda i, j, k: (k, j)),
            pl.BlockSpec((tm, tn), lambda i, j, k: (i, j)),
        ],
        out_specs=pl.BlockSpec((tm, tn), lambda i, j, k: (i, j)),
        out_shape=jax.ShapeDtypeStruct((M, N), F32),
        scratch_shapes=[pltpu.VMEM((tm, tn), F32)] if nk > 1 else [],
        compiler_params=_cparams(("arbitrary", "arbitrary", "arbitrary")),
        name=name,
    )(a, b, res)


def _pad_cols(w, width):
    return jnp.pad(w, ((0, 0), (0, width - w.shape[1])))


def kernel(x, meta_tokens, ln1, w_in, gla_wf, gla_bf, gla_wb, gla_bb, gla_norm, q_norm, w_uq, kv_norm, w_ukv,
           w_gla_out, w_mla_out, w_o, ln2, w_ff1, w_ff2, final_norm):
    assert x.shape[0] == 1 and ln1.shape[0] == 1, "one sequence, one layer"
    S, D = x.shape[1], x.shape[2]
    n_meta = meta_tokens.shape[0]
    rank = gla_wf.shape[1]
    qk_w = gla_wf.shape[2]
    hg = qk_w // GLA_DK
    v_w = hg * GLA_DV
    q_rank, kv_rank = w_uq.shape[1], w_ukv.shape[1]
    hm = w_uq.shape[2] // (MLA_NOPE + MLA_ROPE)
    half = MLA_ROPE // 2
    assert n_meta <= GLA_CHUNK and rank <= LANES and hm % 2 == 0 and S % GLA_CHUNK == 0

    wt = w_in[0].T
    o_gog_end = 2 * qk_w + 2 * v_w
    o_lrb = o_gog_end + rank
    o_cq = o_lrb + rank
    o_ckv = o_cq + q_rank
    o_kr = o_ckv + kv_rank
    o_ga = o_kr + MLA_ROPE
    pad_rows = lambda a: jnp.pad(a, ((0, LANES - a.shape[0]), (0, 0)))
    wt_b = jnp.concatenate([
        pad_rows(wt[o_gog_end:o_lrb]), pad_rows(wt[o_lrb:o_cq]),
        pad_rows(wt[o_kr:o_kr + half]), pad_rows(wt[o_kr + half:o_ga]),
        wt[o_ckv:o_kr], wt[o_cq:o_ckv]], axis=0)
    kr_blk = 1
    ckv_blk = (4 * LANES) // kv_rank
    cq_blk = (4 * LANES + kv_rank) // q_rank
    assert (4 * LANES) % kv_rank == 0 and (4 * LANES + kv_rank) % q_rank == 0

    wq = w_uq[0].reshape(q_rank, hm, MLA_NOPE + MLA_ROPE)
    wq = jnp.concatenate([wq[:, :, :MLA_NOPE].reshape(q_rank, -1), wq[:, :, MLA_NOPE:].reshape(q_rank, -1)],
                         axis=1).astype(BF16)
    wkv = w_ukv[0].astype(BF16)
    wf = jnp.pad(gla_wf[0], ((0, LANES - rank), (0, 0))).astype(BF16)
    wb = jnp.pad(gla_wb[0], ((0, LANES - rank), (0, 0))).astype(BF16)

    inv_freq = ROPE_THETA ** (-jnp.arange(0, MLA_ROPE, 2, dtype=F32) / MLA_ROPE)
    ang = jnp.arange(n_meta + S, dtype=F32)[:, None] * inv_freq[None, :]
    cos, sin = jnp.cos(ang), jnp.sin(ang)
    cos_x, sin_x = cos[n_meta:], sin[n_meta:]
    cos_xp, sin_xp = _pad_cols(cos_x, LANES), _pad_cols(sin_x, LANES)
    mrows = GLA_CHUNK
    padm = lambda a: jnp.pad(a, ((0, mrows - n_meta), (0, LANES - a.shape[1])))
    cos_mp, sin_mp = padm(cos[:n_meta]), padm(sin[:n_meta])

    xs = x[0]
    xm = jnp.pad(meta_tokens.astype(F32), ((0, mrows - n_meta), (0, 0)))
    g1 = ln1[0][None, :]

    w_b = wt_b.T.astype(BF16)
    xn, proj_b = _norm_matmul_emit(xs, g1, w_b, out_dtype=F32, name="in_proj_b")
    xmn, proj_b_m = _norm_matmul_emit(xm, g1, w_b, out_dtype=F32, tm=mrows, name="in_proj_b_meta")
    proj_a = _ws_matmul(xn, wt, col0=0, n_out=o_gog_end, out_dtype=BF16, tm=1024, tn=512, name="in_proj_a")
    gates = _ws_matmul(xn, wt, col0=o_ga, n_out=2 * D, out_dtype=BF16, tm=1024, tn=512, name="in_proj_gates")
    kv_m = _ws_matmul(xmn, wt, col0=qk_w, n_out=qk_w + v_w, out_dtype=BF16, tm=mrows, tn=512,
                      name="in_proj_kv_meta")

    og = _gla(proj_a, proj_b, kv_m[:, :qk_w], kv_m[:, qk_w:], proj_b_m[:, :LANES],
              wf, gla_bf, wb, gla_bb, gla_norm, heads=hg, n_meta=n_meta)

    qt = _q_up(proj_b, cq_blk, q_norm, wq, cos_x.T, sin_x.T, heads=hm)
    gkv = kv_norm
    k_x, vt_x = _kv_up(proj_b, gkv, wkv, cos_xp, sin_xp, heads=hm, ckv_blk=ckv_blk, kr_blk=kr_blk,
                       tm=_tile(S, 512))
    k_m, vt_m = _kv_up(proj_b_m, gkv, wkv, cos_mp, sin_mp, heads=hm, ckv_blk=ckv_blk, kr_blk=kr_blk, tm=mrows)
    om, (wgo, wmo, wo, w1, w2) = _flash(qt, k_x, vt_x, k_m, vt_m,
                                        (w_gla_out[0], w_mla_out[0], w_o[0], w_ff1[0], w_ff2[0]), n_meta=n_meta)

    merged = _merge(og, om, wgo, wmo, gates)
    h1 = _res_matmul(merged, wo, xs, name="o_proj")
    hf = _norm_matmul(h1, ln2[0][None, :], w1, out_dtype=BF16, act="relu2", name="ffn_up")
    h2 = _res_matmul(hf, w2, h1, tk=2048, name="ffn_down")
    return _rms_rows(h2, final_norm[None, :], out_dtype=F32, name="final_norm")[None]
```

```python
import functools
import math

import jax
import jax.numpy as jnp
import numpy as np
from jax import lax
from jax.experimental import pallas as pl
from jax.experimental.pallas import tpu as pltpu

F32 = jnp.float32
BF16 = jnp.bfloat16

GLA_DK = 128
GLA_DV = 256
GLA_GATE_NORM = 16.0
MLA_NOPE = 128
MLA_ROPE = 64
MLA_V = 128
ROPE_THETA = 10000.0
EPS = 1e-6

LANES = 128
GLA_CHUNK = 128
GLA_LEVELS = (64, 32, 16, 8, 4, 2, 1)
NEG_BIG = -1e30
VMEM_LIMIT = 56 * 1024 * 1024


def _cparams(sem):
    return pltpu.CompilerParams(dimension_semantics=sem, vmem_limit_bytes=VMEM_LIMIT)


def _tile(n, pref):
    t = min(n, pref)
    while n % t:
        t //= 2
    return t


def _rms(a, g, eps=EPS):
    ms = jnp.mean(a * a, axis=-1, keepdims=True)
    return a * lax.rsqrt(ms + eps) * g


def _nmm_kernel(a_ref, g_ref, b_ref, o_ref, an_ref, *, act):
    def project(an):
        y = jnp.dot(an, b_ref[...], preferred_element_type=F32)
        if act == "relu2":
            y = jnp.square(jnp.maximum(y, 0.0))
        o_ref[...] = y.astype(o_ref.dtype)

    @pl.when(pl.program_id(1) == 0)
    def _():
        an = _rms(a_ref[...], g_ref[...]).astype(BF16)
        an_ref[...] = an
        project(an)

    @pl.when(pl.program_id(1) > 0)
    def _():
        project(an_ref[...])


def _norm_matmul(a, g, b, *, out_dtype, tm=512, tn=1024, act=None, name):
    M, K = a.shape
    N = b.shape[1]
    tm, tn = _tile(M, tm), _tile(N, tn)
    return pl.pallas_call(
        functools.partial(_nmm_kernel, act=act),
        grid=(M // tm, N // tn),
        in_specs=[
            pl.BlockSpec((tm, K), lambda i, j: (i, 0)),
            pl.BlockSpec((1, K), lambda i, j: (0, 0)),
            pl.BlockSpec((K, tn), lambda i, j: (0, j)),
        ],
        out_specs=pl.BlockSpec((tm, tn), lambda i, j: (i, j)),
        out_shape=jax.ShapeDtypeStruct((M, N), out_dtype),
        scratch_shapes=[pltpu.VMEM((tm, K), BF16)],
        compiler_params=_cparams(("arbitrary", "arbitrary")),
        name=name,
    )(a, g, b)


def _rms_rows_kernel(x_ref, g_ref, o_ref):
    o_ref[...] = _rms(x_ref[...], g_ref[...]).astype(o_ref.dtype)


def _rms_rows(x, g, *, out_dtype, tm=256, name):
    M, D = x.shape
    tm = _tile(M, tm)
    return pl.pallas_call(
        _rms_rows_kernel,
        grid=(M // tm,),
        in_specs=[pl.BlockSpec((tm, D), lambda i: (i, 0)), pl.BlockSpec((1, D), lambda i: (0, 0))],
        out_specs=pl.BlockSpec((tm, D), lambda i: (i, 0)),
        out_shape=jax.ShapeDtypeStruct((M, D), out_dtype),
        compiler_params=_cparams(("arbitrary",)),
        name=name,
    )(x, g)


def _ws_kernel(*refs, shift):
    a_ref, w_ref = refs[0], refs[1]
    w2_ref = refs[2] if shift is not None else None
    o_ref, wb_ref = refs[-2], refs[-1]
    K, tn = wb_ref.shape

    def project():
        o_ref[...] = jnp.dot(a_ref[...], wb_ref[...], preferred_element_type=F32).astype(o_ref.dtype)

    @pl.when(pl.program_id(1) == 0)
    def _():
        kc = _tile(K, 512)
        for r in range(0, K, kc):
            if shift is None:
                blk = w_ref[:, r:r + kc]
            else:
                blk = jnp.concatenate([w_ref[shift:, r:r + kc], w2_ref[:shift, r:r + kc]], axis=0)
            wb_ref[r:r + kc, :] = blk.T.astype(BF16)
        project()

    pl.when(pl.program_id(1) > 0)(project)


def _ws_matmul(a, wt, *, col0, n_out, out_dtype, tm, tn, name):
    M, K = a.shape
    shift = col0 % LANES
    base = col0 - shift
    tm, tn = _tile(M, tm), _tile(math.gcd(n_out, base), tn)
    nj, ni = n_out // tn, M // tm
    in_specs = [pl.BlockSpec((tm, K), lambda j, i: (i, 0)),
                pl.BlockSpec((tn, K), lambda j, i: (j + base // tn, 0))]
    operands = [a, wt]
    if shift:
        in_specs.append(pl.BlockSpec((LANES, K), lambda j, i: ((j + 1) * (tn // LANES) + base // LANES, 0)))
        operands.append(wt)
    return pl.pallas_call(
        functools.partial(_ws_kernel, shift=shift if shift else None),
        grid=(nj, ni),
        in_specs=in_specs,
        out_specs=pl.BlockSpec((tm, tn), lambda j, i: (i, j)),
        out_shape=jax.ShapeDtypeStruct((M, n_out), out_dtype),
        scratch_shapes=[pltpu.VMEM((K, tn), BF16)],
        compiler_params=_cparams(("arbitrary", "arbitrary")),
        name=name,
    )(*operands)


def _q_up_kernel(cq_ref, g_ref, w_ref, cos_ref, sin_ref, qt_ref, *, heads, scale):
    cn = _rms(cq_ref[...], g_ref[...]).astype(BF16)
    y = jnp.dot(cn, w_ref[...], preferred_element_type=F32) * scale
    c = cos_ref[...]
    s = sin_ref[...]
    half = MLA_ROPE // 2
    for h in range(heads):
        qt_ref[h, 0:MLA_NOPE, :] = y[:, h * MLA_NOPE:(h + 1) * MLA_NOPE].T.astype(BF16)
    rope0 = heads * MLA_NOPE
    for p in range(heads // 2):
        blk = y[:, rope0 + p * LANES: rope0 + (p + 1) * LANES].T
        for u in range(2):
            x1 = blk[u * MLA_ROPE: u * MLA_ROPE + half]
            x2 = blk[u * MLA_ROPE + half: (u + 1) * MLA_ROPE]
            h = 2 * p + u
            qt_ref[h, MLA_NOPE:MLA_NOPE + half, :] = (x1 * c - x2 * s).astype(BF16)
            qt_ref[h, MLA_NOPE + half:MLA_NOPE + MLA_ROPE, :] = (x1 * s + x2 * c).astype(BF16)
    pad = qt_ref.shape[1] - MLA_NOPE - MLA_ROPE
    for h in range(heads):
        qt_ref[h, MLA_NOPE + MLA_ROPE:, :] = jnp.zeros((pad, qt_ref.shape[2]), BF16)


def _q_up(proj_b, cq_blk, g, w, cos_t, sin_t, *, heads, tm=512):
    M = proj_b.shape[0]
    R = g.shape[1]
    tm = _tile(M, tm)
    kq = 2 * LANES
    return pl.pallas_call(
        functools.partial(_q_up_kernel, heads=heads, scale=float((MLA_NOPE + MLA_ROPE) ** -0.5 * math.log2(math.e))),
        grid=(M // tm,),
        in_specs=[
            pl.BlockSpec((tm, R), lambda i: (i, cq_blk)),
            pl.BlockSpec((1, R), lambda i: (0, 0)),
            pl.BlockSpec(w.shape, lambda i: (0, 0)),
            pl.BlockSpec((MLA_ROPE // 2, tm), lambda i: (0, i)),
            pl.BlockSpec((MLA_ROPE // 2, tm), lambda i: (0, i)),
        ],
        out_specs=pl.BlockSpec((heads, kq, tm), lambda i: (0, 0, i)),
        out_shape=jax.ShapeDtypeStruct((heads, kq, M), BF16),
        compiler_params=_cparams(("arbitrary",)),
        name="mla_q_up",
    )(proj_b, g, w, cos_t, sin_t)


def _kv_up_kernel(ckv_ref, kr_ref, g_ref, w_ref, cos_ref, sin_ref, k_ref, vt_ref, *, heads):
    cn = _rms(ckv_ref[...], g_ref[...]).astype(BF16)
    y = jnp.dot(cn, w_ref[...], preferred_element_type=F32)
    x1 = kr_ref[:, 0:LANES]
    x2 = kr_ref[:, LANES:2 * LANES]
    c = cos_ref[...]
    s = sin_ref[...]
    kpe = (x1 * c - x2 * s) + pltpu.roll(x1 * s + x2 * c, MLA_ROPE // 2, axis=1)
    kpe = kpe.astype(BF16)
    hw = MLA_NOPE + MLA_V
    for h in range(heads):
        k_ref[h, :, 0:MLA_NOPE] = y[:, h * hw: h * hw + MLA_NOPE].astype(BF16)
        k_ref[h, :, MLA_NOPE:] = kpe
        vt_ref[h, 0] = y[:, h * hw + MLA_NOPE:(h + 1) * hw].T.astype(BF16)


def _kv_up(proj_b, g, w, cos_p, sin_p, *, heads, ckv_blk, kr_blk, tm):
    M = proj_b.shape[0]
    R = g.shape[1]
    return pl.pallas_call(
        functools.partial(_kv_up_kernel, heads=heads),
        grid=(M // tm,),
        in_specs=[
            pl.BlockSpec((tm, R), lambda i: (i, ckv_blk)),
            pl.BlockSpec((tm, 2 * LANES), lambda i: (i, kr_blk)),
            pl.BlockSpec((1, R), lambda i: (0, 0)),
            pl.BlockSpec(w.shape, lambda i: (0, 0)),
            pl.BlockSpec((tm, LANES), lambda i: (i, 0)),
            pl.BlockSpec((tm, LANES), lambda i: (i, 0)),
        ],
        out_specs=[
            pl.BlockSpec((heads, tm, 2 * LANES), lambda i: (0, i, 0)),
            pl.BlockSpec((heads, 1, MLA_V, tm), lambda i: (0, i, 0, 0)),
        ],
        out_shape=[
            jax.ShapeDtypeStruct((heads, M, 2 * LANES), BF16),
            jax.ShapeDtypeStruct((heads, M // tm, MLA_V, tm), BF16),
        ],
        compiler_params=_cparams(("arbitrary",)),
        name="mla_kv_up",
    )(proj_b, proj_b, g, w, cos_p, sin_p)


def _flash_kernel(*refs, nkb, tk, n_meta, streams, n_cast):
    qt_ref, k_ref, vt_ref, km_ref, vtm_ref = refs[:5]
    cast_in = refs[5:5 + n_cast]
    o_ref = refs[5 + n_cast]
    cast_out = refs[6 + n_cast:6 + 2 * n_cast]
    s_ref, p_ref, acc_ref = refs[6 + 2 * n_cast:]
    for ci, co in zip(cast_in, cast_out):
        co[...] = ci[...].astype(BF16)
    ts = qt_ref.shape[1] // streams
    strip = min(tk, 64)
    cols = [slice(c * ts, (c + 1) * ts) for c in range(streams)]

    def scores(c, j, slot):
        s_ref[c, slot] = jnp.dot(k_ref[j * tk:(j + 1) * tk, :], qt_ref[:, cols[c]], preferred_element_type=F32)

    def softmax(c, slot, m, l):
        mx = jnp.full((8, ts), NEG_BIG, F32)
        for r in range(0, tk, strip):
            mx = jnp.maximum(mx, jnp.max(s_ref[c, slot, r:r + strip, :].reshape(strip // 8, 8, ts), axis=0))
        m_new = jnp.maximum(m, jnp.max(mx, axis=0, keepdims=True))
        psum = jnp.zeros((8, ts), F32)
        for r in range(0, tk, strip):
            p = jnp.exp2(s_ref[c, slot, r:r + strip, :] - m_new)
            psum = psum + jnp.sum(p.reshape(strip // 8, 8, ts), axis=0)
            p_ref[c, slot, r:r + strip, :] = p.astype(BF16)
        alpha = jnp.exp2(m - m_new)
        return m_new, alpha * l + jnp.sum(psum, axis=0, keepdims=True), alpha

    def values(c, j, slot, alpha):
        acc_ref[c] = alpha * acc_ref[c] + jnp.dot(vt_ref[j], p_ref[c, slot], preferred_element_type=F32)

    stats = []
    for c in range(streams):
        scores(c, 0, 0)
        s_m = jnp.dot(km_ref[...], qt_ref[:, cols[c]], preferred_element_type=F32)
        rows = lax.broadcasted_iota(jnp.int32, s_m.shape, 0)
        s_m = jnp.where(rows < n_meta, s_m, NEG_BIG)
        m = jnp.max(s_m, axis=0, keepdims=True)
        p_m = jnp.exp2(s_m - m)
        l = jnp.sum(p_m, axis=0, keepdims=True)
        acc_ref[c] = jnp.dot(vtm_ref[0], p_m.astype(BF16), preferred_element_type=F32)
        stats.append((m, l, None))
    for j in range(nkb):
        for c in range(streams):
            m, l, alpha = stats[c]
            if j + 1 < nkb:
                scores(c, j + 1, (j + 1) % 2)
            if j > 0:
                values(c, j - 1, (j - 1) % 2, alpha)
            stats[c] = softmax(c, j % 2, m, l)
    for c in range(streams):
        m, l, alpha = stats[c]
        values(c, nkb - 1, (nkb - 1) % 2, alpha)
        o_ref[cols[c], :] = (acc_ref[c] / l).T.astype(o_ref.dtype)


def _cast_blocks(rows, steps):
    d = max(1, min(rows // 16, steps))
    while (rows // 16) % d:
        d -= 1
    return d


def _flash(qt, k, vt, km, vtm, casts, *, n_meta, tq=1024, streams=4):
    heads, kq, M = qt.shape
    nkb, tk = vt.shape[1], vt.shape[3]
    tq = _tile(M, tq)
    tkm = km.shape[1]
    ts = tq // streams
    nq = M // tq
    cast_specs = []
    for c in casts:
        nblk = _cast_blocks(c.shape[0], heads * nq)
        cast_specs.append(pl.BlockSpec((c.shape[0] // nblk, c.shape[1]),
                                       lambda h, i, nblk=nblk: (jnp.minimum(h * nq + i, nblk - 1), 0)))
    res = pl.pallas_call(
        functools.partial(_flash_kernel, nkb=nkb, tk=tk, n_meta=n_meta, streams=streams, n_cast=len(casts)),
        grid=(heads, nq),
        in_specs=[
            pl.BlockSpec((None, kq, tq), lambda h, i: (h, 0, i)),
            pl.BlockSpec((None, M, kq), lambda h, i: (h, 0, 0)),
            pl.BlockSpec((None, nkb, MLA_V, tk), lambda h, i: (h, 0, 0, 0)),
            pl.BlockSpec((None, tkm, kq), lambda h, i: (h, 0, 0)),
            pl.BlockSpec((None, 1, MLA_V, tkm), lambda h, i: (h, 0, 0, 0)),
        ] + cast_specs,
        out_specs=[pl.BlockSpec((tq, MLA_V), lambda h, i: (i, h))] + cast_specs,
        out_shape=[jax.ShapeDtypeStruct((M, heads * MLA_V), BF16)]
        + [jax.ShapeDtypeStruct(c.shape, BF16) for c in casts],
        scratch_shapes=[pltpu.VMEM((streams, 2, tk, ts), F32), pltpu.VMEM((streams, 2, tk, ts), BF16),
                        pltpu.VMEM((streams, MLA_V, ts), F32)],
        compiler_params=_cparams(("arbitrary", "arbitrary")),
        name="mla_flash",
    )(qt, k, vt, km, vtm, *casts)
    return res[0], tuple(res[1:])


def _gla_consts(direction):
    C = GLA_CHUNK
    r = np.arange(C)
    if direction == "fwd":
        cum = (r[None, :] <= r[:, None])
    else:
        cum = (r[None, :] >= r[:, None])
    eq2 = np.zeros((C, C), np.float32)
    ek2 = np.zeros((C, C), np.float32)
    for t in range(C):
        pos = t % 4
        if direction == "fwd":
            if pos == 2:
                eq2[t, t] = 1
            elif pos == 3:
                eq2[t, t] = 1
                eq2[t, t - 1] = 1
            elif pos == 0:
                ek2[t, t + 1] = 1
        else:
            if pos == 1:
                eq2[t, t] = 1
            elif pos == 0:
                eq2[t, t] = 1
                eq2[t, t + 1] = 1
            elif pos == 3:
                ek2[t, t - 1] = 1
    mats = np.concatenate([cum.astype(np.float32), eq2, ek2], axis=0)
    masks = np.zeros((6, C, LANES), np.float32)
    for li, m in enumerate((4, 2, 1)):
        upper = (r % (2 * m)) >= m
        qrows = upper if direction == "fwd" else ~upper
        masks[2 * li] = np.where(qrows, 0.0, NEG_BIG)[:, None]
        masks[2 * li + 1] = np.where(~qrows, 0.0, NEG_BIG)[:, None]
    return jnp.asarray(mats, BF16), jnp.asarray(masks, F32)


def _split3_dot(mat, x):
    x1 = x.astype(BF16)
    r1 = x - x1.astype(F32)
    x2 = r1.astype(BF16)
    x3 = (r1 - x2.astype(F32)).astype(BF16)
    return (jnp.dot(mat, x1, preferred_element_type=F32)
            + jnp.dot(mat, x2, preferred_element_type=F32)
            + jnp.dot(mat, x3, preferred_element_type=F32))


def _log_decay(lr, w_ref, b_ref):
    z = jnp.dot(lr.astype(BF16), w_ref[...], preferred_element_type=F32) + b_ref[...]
    return jax.nn.log_sigmoid(z) * (math.log2(math.e) / GLA_GATE_NORM)


def _gla_chunk(q, k, v, g, st_ref, c_ref, mats_ref, masks_ref, *, heads, direction, want_out):
    C = GLA_CHUNK
    fwd = direction == "fwd"
    e_all = _split3_dot(mats_ref[...], g)
    c_all = e_all[0:C]
    eq2_all = e_all[C:2 * C]
    ek2_all = e_all[2 * C:3 * C]
    c_ref[...] = c_all
    far = C - 1 if fwd else 0
    outs = []
    for h in range(heads):
        ls = slice(h * GLA_DK, (h + 1) * GLA_DK)
        vs = slice(h * GLA_DV, (h + 1) * GLA_DV)
        kh, gh, ch = k[:, ls], g[:, ls], c_all[:, ls]
        qh = q[:, ls] if want_out else None
        vh = v[:, vs]
        tot = c_ref[pl.ds(far, 1), ls]
        st = st_ref[h]
        if want_out:
            xor = lax.broadcasted_iota(jnp.int32, (C, C), 0) ^ lax.broadcasted_iota(jnp.int32, (C, C), 1)
            dg = jnp.sum(qh * kh, axis=-1, keepdims=True)
            att = jnp.where(xor == 0, dg, 0.0)
            qcat, kcat = [], []
            for m in GLA_LEVELS:
                if m >= 8:
                    qp, kp = [], []
                    for r0 in range(0, C, 2 * m):
                        lo = slice(r0, r0 + m)
                        hi = slice(r0 + m, r0 + 2 * m)
                        ref = c_ref[pl.ds(r0 + m - 1 if fwd else r0 + m, 1), ls]
                        qr, kr = (hi, lo) if fwd else (lo, hi)
                        qa = qh[qr] * jnp.exp2(ch[qr] - ref)
                        ka = kh[kr] * jnp.exp2(ref - ch[kr])
                        z = jnp.zeros((m, GLA_DK), F32)
                        qp += [z, qa] if fwd else [qa, z]
                        kp += [ka, z] if fwd else [z, ka]
                    qcat.append(jnp.concatenate(qp, axis=0).astype(BF16))
                    kcat.append(jnp.concatenate(kp, axis=0).astype(BF16))
                else:
                    li = (4, 2, 1).index(m)
                    mq = masks_ref[2 * li]
                    mk = masks_ref[2 * li + 1]
                    if m == 4:
                        ref = jnp.concatenate(
                            [jnp.broadcast_to(c_ref[pl.ds(r0 + 3 if fwd else r0 + 4, 1), ls], (8, GLA_DK))
                             for r0 in range(0, C, 8)], axis=0)
                        eq = ch - ref
                        ek = ref - ch
                    elif m == 2:
                        eq = eq2_all[:, ls]
                        ek = ek2_all[:, ls]
                    else:
                        eq = gh
                        ek = jnp.zeros_like(gh)
                    qcat.append((qh * jnp.exp2(eq + mq)).astype(BF16))
                    kcat.append((kh * jnp.exp2(ek + mk)).astype(BF16))
                pm = lax.dot_general(qcat.pop(), kcat.pop(), (((1,), (1,)), ((), ())),
                                     preferred_element_type=F32)
                att = att + (pm if 2 * m == C else jnp.where(xor < 2 * m, pm, 0.0))
            qs = (qh * jnp.exp2(ch)).astype(BF16)
            o = jnp.dot(att.astype(BF16), vh, preferred_element_type=F32)
            o = o + lax.dot_general(qs, st.astype(BF16), (((1,), (1,)), ((), ())),
                                    preferred_element_type=F32)
            outs.append(o)
        kd = (kh * jnp.exp2(tot - ch)).astype(BF16)
        ut = lax.dot_general(vh, kd, (((0,), (0,)), ((), ())), preferred_element_type=F32)
        st_ref[h] = st * jnp.exp2(tot) + ut
    return outs if want_out else None


def _gla_fwd_kernel(q_ref, k_ref, v_ref, lr_ref, km_ref, vm_ref, lrm_ref, w_ref, b_ref, mats_ref, masks_ref,
                    o_ref, st_ref, c_ref, *, heads, n_meta, cps):
    n = pl.program_id(0)
    common = dict(heads=heads, direction="fwd")

    @pl.when(n == 0)
    def _():
        st_ref[...] = jnp.zeros_like(st_ref)
        g = _log_decay(lrm_ref[...], w_ref, b_ref)
        rows = lax.broadcasted_iota(jnp.int32, g.shape, 0)
        g = jnp.where(rows < n_meta, g, 0.0)
        k = km_ref[...].astype(F32)
        _gla_chunk(None, k, vm_ref[...], g, st_ref, c_ref.at[0], mats_ref, masks_ref, want_out=False, **common)

    @pl.when(n > 0)
    def _():
        for u in range(cps):
            rows = slice(u * GLA_CHUNK, (u + 1) * GLA_CHUNK)
            g = _log_decay(lr_ref[rows, :], w_ref, b_ref)
            q = q_ref[rows, :].astype(F32) * (GLA_DK ** -0.5)
            k = k_ref[rows, :].astype(F32)
            outs = _gla_chunk(q, k, v_ref[rows, :], g, st_ref, c_ref.at[u], mats_ref, masks_ref,
                              want_out=True, **common)
            for h, o in enumerate(outs):
                o_ref[rows, h * GLA_DV:(h + 1) * GLA_DV] = o


def _gla_bwd_kernel(q_ref, k_ref, v_ref, gog_ref, lr_ref, of_ref, w_ref, b_ref, gn_ref, mats_ref, masks_ref,
                    o_ref, st_ref, c_ref, *, heads, cps):
    @pl.when(pl.program_id(0) == 0)
    def _():
        st_ref[...] = jnp.zeros_like(st_ref)

    gn = gn_ref[...]
    for u in reversed(range(cps)):
        rows = slice(u * GLA_CHUNK, (u + 1) * GLA_CHUNK)
        g = _log_decay(lr_ref[rows, :], w_ref, b_ref)
        q = q_ref[rows, :].astype(F32) * (GLA_DK ** -0.5)
        k = k_ref[rows, :].astype(F32)
        outs = _gla_chunk(q, k, v_ref[rows, :], g, st_ref, c_ref.at[u], mats_ref, masks_ref,
                          heads=heads, direction="bwd", want_out=True)
        for h, ob in enumerate(outs):
            vs = slice(h * GLA_DV, (h + 1) * GLA_DV)
            o = _rms(of_ref[rows, vs] + ob, gn)
            o_ref[rows, vs] = (o * jax.nn.silu(gog_ref[rows, vs].astype(F32))).astype(o_ref.dtype)


def _gla(proj_ac, proj_b, km, vm, lrm, wf, bf, wb, bb, gn, *, heads, n_meta, cps=4):
    M = proj_ac.shape[0]
    cps = cps if M % (cps * GLA_CHUNK) == 0 else 1
    C = cps * GLA_CHUNK
    nch = M // C
    qw, vw = heads * GLA_DK, heads * GLA_DV
    const2 = lambda n: (0, 0)
    const3 = lambda n: (0, 0, 0)
    scratch = [pltpu.VMEM((heads, GLA_DV, GLA_DK), F32), pltpu.VMEM((cps, GLA_CHUNK, qw), F32)]
    mats_f, masks_f = _gla_consts("fwd")
    mats_b, masks_b = _gla_consts("bwd")

    xb = lambda n: jnp.maximum(n - 1, 0)
    o_f = pl.pallas_call(
        functools.partial(_gla_fwd_kernel, heads=heads, n_meta=n_meta, cps=cps),
        grid=(nch + 1,),
        in_specs=[
            pl.BlockSpec((C, qw), lambda n: (xb(n), 0)),
            pl.BlockSpec((C, qw), lambda n: (xb(n), 1)),
            pl.BlockSpec((C, vw), lambda n: (xb(n), 1)),
            pl.BlockSpec((C, LANES), lambda n: (xb(n), 0)),
            pl.BlockSpec(km.shape, const2),
            pl.BlockSpec(vm.shape, const2),
            pl.BlockSpec(lrm.shape, const2),
            pl.BlockSpec(wf.shape, const2),
            pl.BlockSpec(bf.shape, const2),
            pl.BlockSpec(mats_f.shape, const2),
            pl.BlockSpec(masks_f.shape, const3),
        ],
        out_specs=pl.BlockSpec((C, vw), lambda n: (xb(n), 0)),
        out_shape=jax.ShapeDtypeStruct((M, vw), F32),
        scratch_shapes=scratch,
        compiler_params=_cparams(("arbitrary",)),
        name="gla_fwd",
    )(proj_ac, proj_ac, proj_ac, proj_b, km, vm, lrm, wf, bf, mats_f, masks_f)

    rb = lambda n: nch - 1 - n
    return pl.pallas_call(
        functools.partial(_gla_bwd_kernel, heads=heads, cps=cps),
        grid=(nch,),
        in_specs=[
            pl.BlockSpec((C, qw), lambda n: (rb(n), 0)),
            pl.BlockSpec((C, qw), lambda n: (rb(n), 1)),
            pl.BlockSpec((C, vw), lambda n: (rb(n), 1)),
            pl.BlockSpec((C, vw), lambda n: (rb(n), 2)),
            pl.BlockSpec((C, LANES), lambda n: (rb(n), 1)),
            pl.BlockSpec((C, vw), lambda n: (rb(n), 0)),
            pl.BlockSpec(wb.shape, const2),
            pl.BlockSpec(bb.shape, const2),
            pl.BlockSpec(gn.shape, const2),
            pl.BlockSpec(mats_b.shape, const2),
            pl.BlockSpec(masks_b.shape, const3),
        ],
        out_specs=pl.BlockSpec((C, vw), lambda n: (rb(n), 0)),
        out_shape=jax.ShapeDtypeStruct((M, vw), BF16),
        scratch_shapes=scratch,
        compiler_params=_cparams(("arbitrary",)),
        name="gla_bwd",
    )(proj_ac, proj_ac, proj_ac, proj_ac, proj_b, o_f, wb, bb, gn, mats_b, masks_b)


def _merge_kernel(og_ref, om_ref, wg_ref, wm_ref, ga_ref, gb_ref, o_ref):
    yg = jnp.dot(og_ref[...], wg_ref[...], preferred_element_type=F32)
    ym = jnp.dot(om_ref[...], wm_ref[...], preferred_element_type=F32)
    ga = jax.nn.sigmoid(ga_ref[...].astype(F32))
    gb = jax.nn.sigmoid(gb_ref[...].astype(F32))
    o_ref[...] = (ga * yg + gb * ym).astype(o_ref.dtype)


def _merge(og, om, wg, wm, gates, *, tm=1024, tn=1024):
    M, D = og.shape[0], wg.shape[1]
    tm, tn = _tile(M, tm), _tile(D, tn)
    gb_blk = D // tn
    return pl.pallas_call(
        _merge_kernel,
        grid=(M // tm, D // tn),
        in_specs=[
            pl.BlockSpec((tm, og.shape[1]), lambda i, j: (i, 0)),
            pl.BlockSpec((tm, om.shape[1]), lambda i, j: (i, 0)),
            pl.BlockSpec((wg.shape[0], tn), lambda i, j: (0, j)),
            pl.BlockSpec((wm.shape[0], tn), lambda i, j: (0, j)),
            pl.BlockSpec((tm, tn), lambda i, j: (i, j)),
            pl.BlockSpec((tm, tn), lambda i, j: (i, j + gb_blk)),
        ],
        out_specs=pl.BlockSpec((tm, tn), lambda i, j: (i, j)),
        out_shape=jax.ShapeDtypeStruct((M, D), BF16),
        compiler_params=_cparams(("arbitrary", "arbitrary")),
        name="merge_out_proj",
    )(og, om, wg, wm, gates, gates)


def _rmm_kernel(a_ref, b_ref, r_ref, o_ref, *acc, nk):
    def part():
        return jnp.dot(a_ref[...], b_ref[...], preferred_element_type=F32)

    if nk == 1:
        o_ref[...] = r_ref[...] + part()
        return
    acc_ref, = acc
    kk = pl.program_id(2)

    @pl.when(kk == 0)
    def _():
        acc_ref[...] = part()

    @pl.when((kk > 0) & (kk < nk - 1))
    def _():
        acc_ref[...] += part()

    @pl.when(kk == nk - 1)
    def _():
        o_ref[...] = r_ref[...] + (acc_ref[...] + part())


def _res_matmul(a, b, res, *, tm=1024, tn=1024, tk=4096, name):
    M, K = a.shape
    N = b.shape[1]
    tm, tn, tk = _tile(M, tm), _tile(N, tn), _tile(K, tk)
    nk = K // tk
    return pl.pallas_call(
        functools.partial(_rmm_kernel, nk=nk),
        grid=(M // tm, N // tn, nk),
        in_specs=[
            pl.BlockSpec((tm, tk), lambda i, j, k: (i, k)),
            pl.BlockSpec((tk, tn), lambda i, j, k: (k, j)),
            pl.BlockSpec((tm, tn), lambda i, j, k: (i, j)),
        ],
        out_specs=pl.BlockSpec((tm, tn), lambda i, j, k: (i, j)),
        out_shape=jax.ShapeDtypeStruct((M, N), F32),
        scratch_shapes=[pltpu.VMEM((tm, tn), F32)] if nk > 1 else [],
        compiler_params=_cparams(("arbitrary", "arbitrary", "arbitrary")),
        name=name,
    )(a, b, res)


def _pad_cols(w, width):
    return jnp.pad(w, ((0, 0), (0, width - w.shape[1])))


def kernel(x, meta_tokens, ln1, w_in, gla_wf, gla_bf, gla_wb, gla_bb, gla_norm, q_norm, w_uq, kv_norm, w_ukv,
           w_gla_out, w_mla_out, w_o, ln2, w_ff1, w_ff2, final_norm):
    assert x.shape[0] == 1 and ln1.shape[0] == 1, "one sequence, one layer"
    S, D = x.shape[1], x.shape[2]
    n_meta = meta_tokens.shape[0]
    rank = gla_wf.shape[1]
    qk_w = gla_wf.shape[2]
    hg = qk_w // GLA_DK
    v_w = hg * GLA_DV
    q_rank, kv_rank = w_uq.shape[1], w_ukv.shape[1]
    hm = w_uq.shape[2] // (MLA_NOPE + MLA_ROPE)
    half = MLA_ROPE // 2
    assert n_meta <= GLA_CHUNK and rank <= LANES and hm % 2 == 0 and S % GLA_CHUNK == 0

    wt = w_in[0].T
    o_gog_end = 2 * qk_w + 2 * v_w
    o_lrb = o_gog_end + rank
    o_cq = o_lrb + rank
    o_ckv = o_cq + q_rank
    o_kr = o_ckv + kv_rank
    o_ga = o_kr + MLA_ROPE
    pad_rows = lambda a: jnp.pad(a, ((0, LANES - a.shape[0]), (0, 0)))
    wt_b = jnp.concatenate([
        pad_rows(wt[o_gog_end:o_lrb]), pad_rows(wt[o_lrb:o_cq]),
        pad_rows(wt[o_kr:o_kr + half]), pad_rows(wt[o_kr + half:o_ga]),
        wt[o_ckv:o_kr], wt[o_cq:o_ckv]], axis=0)
    kr_blk = 1
    ckv_blk = (4 * LANES) // kv_rank
    cq_blk = (4 * LANES + kv_rank) // q_rank
    assert (4 * LANES) % kv_rank == 0 and (4 * LANES + kv_rank) % q_rank == 0

    wq = w_uq[0].reshape(q_rank, hm, MLA_NOPE + MLA_ROPE)
    wq = jnp.concatenate([wq[:, :, :MLA_NOPE].reshape(q_rank, -1), wq[:, :, MLA_NOPE:].reshape(q_rank, -1)],
                         axis=1).astype(BF16)
    wkv = w_ukv[0].astype(BF16)
    wf = jnp.pad(gla_wf[0], ((0, LANES - rank), (0, 0))).astype(BF16)
    wb = jnp.pad(gla_wb[0], ((0, LANES - rank), (0, 0))).astype(BF16)

    inv_freq = ROPE_THETA ** (-jnp.arange(0, MLA_ROPE, 2, dtype=F32) / MLA_ROPE)
    ang = jnp.arange(n_meta + S, dtype=F32)[:, None] * inv_freq[None, :]
    cos, sin = jnp.cos(ang), jnp.sin(ang)
    cos_x, sin_x = cos[n_meta:], sin[n_meta:]
    cos_xp, sin_xp = _pad_cols(cos_x, LANES), _pad_cols(sin_x, LANES)
    mrows = GLA_CHUNK
    padm = lambda a: jnp.pad(a, ((0, mrows - n_meta), (0, LANES - a.shape[1])))
    cos_mp, sin_mp = padm(cos[:n_meta]), padm(sin[:n_meta])

    xs = x[0]
    xm = jnp.pad(meta_tokens.astype(F32), ((0, mrows - n_meta), (0, 0)))
    g1 = ln1[0][None, :]

    xn = _rms_rows(xs, g1, out_dtype=BF16, name="ln1_x")
    xmn = _rms_rows(xm, g1, out_dtype=BF16, name="ln1_meta")
    proj_a = _ws_matmul(xn, wt, col0=0, n_out=o_gog_end, out_dtype=BF16, tm=512, tn=1024, name="in_proj_a")
    gates = _ws_matmul(xn, wt, col0=o_ga, n_out=2 * D, out_dtype=BF16, tm=1024, tn=512, name="in_proj_gates")
    nb = wt_b.shape[0]
    proj_b = _ws_matmul(xn, wt_b, col0=0, n_out=nb, out_dtype=F32, tm=1024, tn=512, name="in_proj_b")
    proj_b_m = _ws_matmul(xmn, wt_b, col0=0, n_out=nb, out_dtype=F32, tm=mrows, tn=512, name="in_proj_b_meta")
    kv_m = _ws_matmul(xmn, wt, col0=qk_w, n_out=qk_w + v_w, out_dtype=BF16, tm=mrows, tn=512,
                      name="in_proj_kv_meta")

    og = _gla(proj_a, proj_b, kv_m[:, :qk_w], kv_m[:, qk_w:], proj_b_m[:, :LANES],
              wf, gla_bf, wb, gla_bb, gla_norm, heads=hg, n_meta=n_meta)

    qt = _q_up(proj_b, cq_blk, q_norm, wq, cos_x.T, sin_x.T, heads=hm)
    gkv = kv_norm
    k_x, vt_x = _kv_up(proj_b, gkv, wkv, cos_xp, sin_xp, heads=hm, ckv_blk=ckv_blk, kr_blk=kr_blk,
                       tm=_tile(S, 512))
    k_m, vt_m = _kv_up(proj_b_m, gkv, wkv, cos_mp, sin_mp, heads=hm, ckv_blk=ckv_blk, kr_blk=kr_blk, tm=mrows)
    om, (wgo, wmo, wo, w1, w2) = _flash(qt, k_x, vt_x, k_m, vt_m,
                                        (w_gla_out[0], w_mla_out[0], w_o[0], w_ff1[0], w_ff2[0]), n_meta=n_meta)

    merged = _merge(og, om, wgo, wmo, gates)
    h1 = _res_matmul(merged, wo, xs, name="o_proj")
    hf = _norm_matmul(h1, ln2[0][None, :], w1, out_dtype=BF16, act="relu2", name="ffn_up")
    h2 = _res_matmul(hf, w2, h1, tk=2048, name="ffn_down")
    return _rms_rows(h2, final_norm[None, :], out_dtype=F32, name="final_norm")[None]
```

```python
import functools
import math

import jax
import jax.numpy as jnp
import numpy as np
from jax import lax
from jax.experimental import pallas as pl
from jax.experimental.pallas import tpu as pltpu

F32 = jnp.float32
BF16 = jnp.bfloat16

GLA_DK = 128
GLA_DV = 256
GLA_GATE_NORM = 16.0
MLA_NOPE = 128
MLA_ROPE = 64
MLA_V = 128
ROPE_THETA = 10000.0
EPS = 1e-6

LANES = 128
GLA_CHUNK = 128
GLA_LEVELS = (64, 32, 16, 8, 4, 2, 1)
NEG_BIG = -1e30
VMEM_LIMIT = 56 * 1024 * 1024


def _cparams(sem):
    return pltpu.CompilerParams(dimension_semantics=sem, vmem_limit_bytes=VMEM_LIMIT)


def _tile(n, pref):
    t = min(n, pref)
    while n % t:
        t //= 2
    return t


def _rms(a, g, eps=EPS):
    ms = jnp.mean(a * a, axis=-1, keepdims=True)
    return a * lax.rsqrt(ms + eps) * g


def _nmm_kernel(a_ref, g_ref, b_ref, o_ref, an_ref, *, act):
    def project(an):
        y = jnp.dot(an, b_ref[...], preferred_element_type=F32)
        if act == "relu2":
            y = jnp.square(jnp.maximum(y, 0.0))
        o_ref[...] = y.astype(o_ref.dtype)

    @pl.when(pl.program_id(1) == 0)
    def _():
        an = _rms(a_ref[...], g_ref[...]).astype(BF16)
        an_ref[...] = an
        project(an)

    @pl.when(pl.program_id(1) > 0)
    def _():
        project(an_ref[...])


def _norm_matmul(a, g, b, *, out_dtype, tm=1024, tn=1024, act=None, name):
    M, K = a.shape
    N = b.shape[1]
    tm, tn = _tile(M, tm), _tile(N, tn)
    return pl.pallas_call(
        functools.partial(_nmm_kernel, act=act),
        grid=(M // tm, N // tn),
        in_specs=[
            pl.BlockSpec((tm, K), lambda i, j: (i, 0), pipeline_mode=pl.Buffered(1)),
            pl.BlockSpec((1, K), lambda i, j: (0, 0)),
            pl.BlockSpec((K, tn), lambda i, j: (0, j)),
        ],
        out_specs=pl.BlockSpec((tm, tn), lambda i, j: (i, j)),
        out_shape=jax.ShapeDtypeStruct((M, N), out_dtype),
        scratch_shapes=[pltpu.VMEM((tm, K), BF16)],
        compiler_params=_cparams(("arbitrary", "arbitrary")),
        name=name,
    )(a, g, b)


def _rms_rows_kernel(x_ref, g_ref, o_ref):
    o_ref[...] = _rms(x_ref[...], g_ref[...]).astype(o_ref.dtype)


def _rms_rows(x, g, *, out_dtype, tm=256, name):
    M, D = x.shape
    tm = _tile(M, tm)
    return pl.pallas_call(
        _rms_rows_kernel,
        grid=(M // tm,),
        in_specs=[pl.BlockSpec((tm, D), lambda i: (i, 0)), pl.BlockSpec((1, D), lambda i: (0, 0))],
        out_specs=pl.BlockSpec((tm, D), lambda i: (i, 0)),
        out_shape=jax.ShapeDtypeStruct((M, D), out_dtype),
        compiler_params=_cparams(("arbitrary",)),
        name=name,
    )(x, g)


def _ws_kernel(*refs, shift):
    a_ref, w_ref = refs[0], refs[1]
    w2_ref = refs[2] if shift is not None else None
    o_ref, wb_ref = refs[-2], refs[-1]
    K, tn = wb_ref.shape

    def project():
        o_ref[...] = jnp.dot(a_ref[...], wb_ref[...], preferred_element_type=F32).astype(o_ref.dtype)

    @pl.when(pl.program_id(1) == 0)
    def _():
        kc = _tile(K, 512)
        for r in range(0, K, kc):
            if shift is None:
                blk = w_ref[:, r:r + kc]
            else:
                blk = jnp.concatenate([w_ref[shift:, r:r + kc], w2_ref[:shift, r:r + kc]], axis=0)
            wb_ref[r:r + kc, :] = blk.T.astype(BF16)
        project()

    pl.when(pl.program_id(1) > 0)(project)


def _ws_matmul(a, wt, *, col0, n_out, out_dtype, tm, tn, name):
    M, K = a.shape
    shift = col0 % LANES
    base = col0 - shift
    tm, tn = _tile(M, tm), _tile(math.gcd(n_out, base), tn)
    nj, ni = n_out // tn, M // tm
    in_specs = [pl.BlockSpec((tm, K), lambda j, i: (i, 0)),
                pl.BlockSpec((tn, K), lambda j, i: (j + base // tn, 0))]
    operands = [a, wt]
    if shift:
        in_specs.append(pl.BlockSpec((LANES, K), lambda j, i: ((j + 1) * (tn // LANES) + base // LANES, 0)))
        operands.append(wt)
    return pl.pallas_call(
        functools.partial(_ws_kernel, shift=shift if shift else None),
        grid=(nj, ni),
        in_specs=in_specs,
        out_specs=pl.BlockSpec((tm, tn), lambda j, i: (i, j)),
        out_shape=jax.ShapeDtypeStruct((M, n_out), out_dtype),
        scratch_shapes=[pltpu.VMEM((K, tn), BF16)],
        compiler_params=_cparams(("arbitrary", "arbitrary")),
        name=name,
    )(*operands)


def _q_up_kernel(cq_ref, g_ref, w_ref, cos_ref, sin_ref, qt_ref, *, heads, scale):
    cn = _rms(cq_ref[...], g_ref[...]).astype(BF16)
    y = jnp.dot(cn, w_ref[...], preferred_element_type=F32) * scale
    c = cos_ref[...]
    s = sin_ref[...]
    half = MLA_ROPE // 2
    for h in range(heads):
        qt_ref[h, 0:MLA_NOPE, :] = y[:, h * MLA_NOPE:(h + 1) * MLA_NOPE].T.astype(BF16)
    rope0 = heads * MLA_NOPE
    for p in range(heads // 2):
        blk = y[:, rope0 + p * LANES: rope0 + (p + 1) * LANES].T
        for u in range(2):
            x1 = blk[u * MLA_ROPE: u * MLA_ROPE + half]
            x2 = blk[u * MLA_ROPE + half: (u + 1) * MLA_ROPE]
            h = 2 * p + u
            qt_ref[h, MLA_NOPE:MLA_NOPE + half, :] = (x1 * c - x2 * s).astype(BF16)
            qt_ref[h, MLA_NOPE + half:MLA_NOPE + MLA_ROPE, :] = (x1 * s + x2 * c).astype(BF16)
    pad = qt_ref.shape[1] - MLA_NOPE - MLA_ROPE
    for h in range(heads):
        qt_ref[h, MLA_NOPE + MLA_ROPE:, :] = jnp.zeros((pad, qt_ref.shape[2]), BF16)


def _q_up(proj_b, cq_blk, g, w, cos_t, sin_t, *, heads, tm=512):
    M = proj_b.shape[0]
    R = g.shape[1]
    tm = _tile(M, tm)
    kq = 2 * LANES
    return pl.pallas_call(
        functools.partial(_q_up_kernel, heads=heads, scale=float((MLA_NOPE + MLA_ROPE) ** -0.5 * math.log2(math.e))),
        grid=(M // tm,),
        in_specs=[
            pl.BlockSpec((tm, R), lambda i: (i, cq_blk)),
            pl.BlockSpec((1, R), lambda i: (0, 0)),
            pl.BlockSpec(w.shape, lambda i: (0, 0)),
            pl.BlockSpec((MLA_ROPE // 2, tm), lambda i: (0, i)),
            pl.BlockSpec((MLA_ROPE // 2, tm), lambda i: (0, i)),
        ],
        out_specs=pl.BlockSpec((heads, kq, tm), lambda i: (0, 0, i)),
        out_shape=jax.ShapeDtypeStruct((heads, kq, M), BF16),
        compiler_params=_cparams(("arbitrary",)),
        name="mla_q_up",
    )(proj_b, g, w, cos_t, sin_t)


def _kv_up_kernel(ckv_ref, kr_ref, g_ref, w_ref, cos_ref, sin_ref, k_ref, vt_ref, *, heads):
    cn = _rms(ckv_ref[...], g_ref[...]).astype(BF16)
    y = jnp.dot(cn, w_ref[...], preferred_element_type=F32)
    x1 = kr_ref[:, 0:LANES]
    x2 = kr_ref[:, LANES:2 * LANES]
    c = cos_ref[...]
    s = sin_ref[...]
    kpe = (x1 * c - x2 * s) + pltpu.roll(x1 * s + x2 * c, MLA_ROPE // 2, axis=1)
    kpe = kpe.astype(BF16)
    hw = MLA_NOPE + MLA_V
    for h in range(heads):
        k_ref[h, :, 0:MLA_NOPE] = y[:, h * hw: h * hw + MLA_NOPE].astype(BF16)
        k_ref[h, :, MLA_NOPE:] = kpe
        vt_ref[h, 0] = y[:, h * hw + MLA_NOPE:(h + 1) * hw].T.astype(BF16)


def _kv_up(proj_b, g, w, cos_p, sin_p, *, heads, ckv_blk, kr_blk, tm):
    M = proj_b.shape[0]
    R = g.shape[1]
    return pl.pallas_call(
        functools.partial(_kv_up_kernel, heads=heads),
        grid=(M // tm,),
        in_specs=[
            pl.BlockSpec((tm, R), lambda i: (i, ckv_blk)),
            pl.BlockSpec((tm, 2 * LANES), lambda i: (i, kr_blk)),
            pl.BlockSpec((1, R), lambda i: (0, 0)),
            pl.BlockSpec(w.shape, lambda i: (0, 0)),
            pl.BlockSpec((tm, LANES), lambda i: (i, 0)),
            pl.BlockSpec((tm, LANES), lambda i: (i, 0)),
        ],
        out_specs=[
            pl.BlockSpec((heads, tm, 2 * LANES), lambda i: (0, i, 0)),
            pl.BlockSpec((heads, 1, MLA_V, tm), lambda i: (0, i, 0, 0)),
        ],
        out_shape=[
            jax.ShapeDtypeStruct((heads, M, 2 * LANES), BF16),
            jax.ShapeDtypeStruct((heads, M // tm, MLA_V, tm), BF16),
        ],
        compiler_params=_cparams(("arbitrary",)),
        name="mla_kv_up",
    )(proj_b, proj_b, g, w, cos_p, sin_p)


def _flash_kernel(*refs, nkb, tk, n_meta, streams, n_cast):
    qt_ref, k_ref, vt_ref, km_ref, vtm_ref = refs[:5]
    cast_in = refs[5:5 + n_cast]
    o_ref = refs[5 + n_cast]
    cast_out = refs[6 + n_cast:6 + 2 * n_cast]
    s_ref, p_ref, acc_ref = refs[6 + 2 * n_cast:]
    for ci, co in zip(cast_in, cast_out):
        co[...] = ci[...].astype(BF16)
    ts = qt_ref.shape[1] // streams
    strip = min(tk, 64)
    cols = [slice(c * ts, (c + 1) * ts) for c in range(streams)]

    def scores(c, j, slot):
        s_ref[c, slot] = jnp.dot(k_ref[j * tk:(j + 1) * tk, :], qt_ref[:, cols[c]], preferred_element_type=F32)

    def softmax(c, slot, m, l):
        mx = jnp.full((8, ts), NEG_BIG, F32)
        for r in range(0, tk, strip):
            mx = jnp.maximum(mx, jnp.max(s_ref[c, slot, r:r + strip, :].reshape(strip // 8, 8, ts), axis=0))
        m_new = jnp.maximum(m, jnp.max(mx, axis=0, keepdims=True))
        psum = jnp.zeros((8, ts), F32)
        for r in range(0, tk, strip):
            p = jnp.exp2(s_ref[c, slot, r:r + strip, :] - m_new)
            psum = psum + jnp.sum(p.reshape(strip // 8, 8, ts), axis=0)
            p_ref[c, slot, r:r + strip, :] = p.astype(BF16)
        alpha = jnp.exp2(m - m_new)
        return m_new, alpha * l + jnp.sum(psum, axis=0, keepdims=True), alpha

    def values(c, j, slot, alpha):
        acc_ref[c] = alpha * acc_ref[c] + jnp.dot(vt_ref[j], p_ref[c, slot], preferred_element_type=F32)

    stats = []
    for c in range(streams):
        scores(c, 0, 0)
        s_m = jnp.dot(km_ref[...], qt_ref[:, cols[c]], preferred_element_type=F32)
        rows = lax.broadcasted_iota(jnp.int32, s_m.shape, 0)
        s_m = jnp.where(rows < n_meta, s_m, NEG_BIG)
        m = jnp.max(s_m, axis=0, keepdims=True)
        p_m = jnp.exp2(s_m - m)
        l = jnp.sum(p_m, axis=0, keepdims=True)
        acc_ref[c] = jnp.dot(vtm_ref[0], p_m.astype(BF16), preferred_element_type=F32)
        stats.append((m, l, None))
    for j in range(nkb):
        for c in range(streams):
            m, l, alpha = stats[c]
            if j + 1 < nkb:
                scores(c, j + 1, (j + 1) % 2)
            if j > 0:
                values(c, j - 1, (j - 1) % 2, alpha)
            stats[c] = softmax(c, j % 2, m, l)
    for c in range(streams):
        m, l, alpha = stats[c]
        values(c, nkb - 1, (nkb - 1) % 2, alpha)
        o_ref[cols[c], :] = (acc_ref[c] / l).T.astype(o_ref.dtype)


def _cast_blocks(rows, steps):
    d = max(1, min(rows // 16, steps))
    while (rows // 16) % d:
        d -= 1
    return d


def _flash(qt, k, vt, km, vtm, casts, *, n_meta, tq=1024, streams=4):
    heads, kq, M = qt.shape
    nkb, tk = vt.shape[1], vt.shape[3]
    tq = _tile(M, tq)
    tkm = km.shape[1]
    ts = tq // streams
    nq = M // tq
    cast_specs = []
    for c in casts:
        nblk = _cast_blocks(c.shape[0], heads * nq)
        cast_specs.append(pl.BlockSpec((c.shape[0] // nblk, c.shape[1]),
                                       lambda h, i, nblk=nblk: (jnp.minimum(h * nq + i, nblk - 1), 0)))
    res = pl.pallas_call(
        functools.partial(_flash_kernel, nkb=nkb, tk=tk, n_meta=n_meta, streams=streams, n_cast=len(casts)),
        grid=(heads, nq),
        in_specs=[
            pl.BlockSpec((None, kq, tq), lambda h, i: (h, 0, i)),
            pl.BlockSpec((None, M, kq), lambda h, i: (h, 0, 0)),
            pl.BlockSpec((None, nkb, MLA_V, tk), lambda h, i: (h, 0, 0, 0)),
            pl.BlockSpec((None, tkm, kq), lambda h, i: (h, 0, 0)),
            pl.BlockSpec((None, 1, MLA_V, tkm), lambda h, i: (h, 0, 0, 0)),
        ] + cast_specs,
        out_specs=[pl.BlockSpec((tq, MLA_V), lambda h, i: (i, h))] + cast_specs,
        out_shape=[jax.ShapeDtypeStruct((M, heads * MLA_V), BF16)]
        + [jax.ShapeDtypeStruct(c.shape, BF16) for c in casts],
        scratch_shapes=[pltpu.VMEM((streams, 2, tk, ts), F32), pltpu.VMEM((streams, 2, tk, ts), BF16),
                        pltpu.VMEM((streams, MLA_V, ts), F32)],
        compiler_params=_cparams(("arbitrary", "arbitrary")),
        name="mla_flash",
    )(qt, k, vt, km, vtm, *casts)
    return res[0], tuple(res[1:])


def _gla_consts(direction):
    C = GLA_CHUNK
    r = np.arange(C)
    if direction == "fwd":
        cum = (r[None, :] <= r[:, None])
    else:
        cum = (r[None, :] >= r[:, None])
    eq2 = np.zeros((C, C), np.float32)
    ek2 = np.zeros((C, C), np.float32)
    for t in range(C):
        pos = t % 4
        if direction == "fwd":
            if pos == 2:
                eq2[t, t] = 1
            elif pos == 3:
                eq2[t, t] = 1
                eq2[t, t - 1] = 1
            elif pos == 0:
                ek2[t, t + 1] = 1
        else:
            if pos == 1:
                eq2[t, t] = 1
            elif pos == 0:
                eq2[t, t] = 1
                eq2[t, t + 1] = 1
            elif pos == 3:
                ek2[t, t - 1] = 1
    mats = np.concatenate([cum.astype(np.float32), eq2, ek2], axis=0)
    masks = np.zeros((6, C, LANES), np.float32)
    for li, m in enumerate((4, 2, 1)):
        upper = (r % (2 * m)) >= m
        qrows = upper if direction == "fwd" else ~upper
        masks[2 * li] = np.where(qrows, 0.0, NEG_BIG)[:, None]
        masks[2 * li + 1] = np.where(~qrows, 0.0, NEG_BIG)[:, None]
    return jnp.asarray(mats, BF16), jnp.asarray(masks, F32)


def _split3_dot(mat, x):
    x1 = x.astype(BF16)
    r1 = x - x1.astype(F32)
    x2 = r1.astype(BF16)
    x3 = (r1 - x2.astype(F32)).astype(BF16)
    return (jnp.dot(mat, x1, preferred_element_type=F32)
            + jnp.dot(mat, x2, preferred_element_type=F32)
            + jnp.dot(mat, x3, preferred_element_type=F32))


def _log_decay(lr, w_ref, b_ref):
    z = jnp.dot(lr.astype(BF16), w_ref[...], preferred_element_type=F32) + b_ref[...]
    return jax.nn.log_sigmoid(z) * (math.log2(math.e) / GLA_GATE_NORM)


def _gla_chunk(q, k, v, g, st_ref, c_ref, mats_ref, masks_ref, *, heads, direction, want_out):
    C = GLA_CHUNK
    fwd = direction == "fwd"
    e_all = _split3_dot(mats_ref[...], g)
    c_all = e_all[0:C]
    eq2_all = e_all[C:2 * C]
    ek2_all = e_all[2 * C:3 * C]
    c_ref[...] = c_all
    far = C - 1 if fwd else 0
    outs = []
    for h in range(heads):
        ls = slice(h * GLA_DK, (h + 1) * GLA_DK)
        vs = slice(h * GLA_DV, (h + 1) * GLA_DV)
        kh, gh, ch = k[:, ls], g[:, ls], c_all[:, ls]
        qh = q[:, ls] if want_out else None
        vh = v[:, vs]
        tot = c_ref[pl.ds(far, 1), ls]
        st = st_ref[h]
        if want_out:
            xor = lax.broadcasted_iota(jnp.int32, (C, C), 0) ^ lax.broadcasted_iota(jnp.int32, (C, C), 1)
            dg = jnp.sum(qh * kh, axis=-1, keepdims=True)
            att = jnp.where(xor == 0, dg, 0.0)
            qcat, kcat = [], []
            for m in GLA_LEVELS:
                if m >= 8:
                    qp, kp = [], []
                    for r0 in range(0, C, 2 * m):
                        lo = slice(r0, r0 + m)
                        hi = slice(r0 + m, r0 + 2 * m)
                        ref = c_ref[pl.ds(r0 + m - 1 if fwd else r0 + m, 1), ls]
                        qr, kr = (hi, lo) if fwd else (lo, hi)
                        qa = qh[qr] * jnp.exp2(ch[qr] - ref)
                        ka = kh[kr] * jnp.exp2(ref - ch[kr])
                        z = jnp.zeros((m, GLA_DK), F32)
                        qp += [z, qa] if fwd else [qa, z]
                        kp += [ka, z] if fwd else [z, ka]
                    qcat.append(jnp.concatenate(qp, axis=0).astype(BF16))
                    kcat.append(jnp.concatenate(kp, axis=0).astype(BF16))
                else:
                    li = (4, 2, 1).index(m)
                    mq = masks_ref[2 * li]
                    mk = masks_ref[2 * li + 1]
                    if m == 4:
                        ref = jnp.concatenate(
                            [jnp.broadcast_to(c_ref[pl.ds(r0 + 3 if fwd else r0 + 4, 1), ls], (8, GLA_DK))
                             for r0 in range(0, C, 8)], axis=0)
                        eq = ch - ref
                        ek = ref - ch
                    elif m == 2:
                        eq = eq2_all[:, ls]
                        ek = ek2_all[:, ls]
                    else:
                        eq = gh
                        ek = jnp.zeros_like(gh)
                    qcat.append((qh * jnp.exp2(eq + mq)).astype(BF16))
                    kcat.append((kh * jnp.exp2(ek + mk)).astype(BF16))
                pm = lax.dot_general(qcat.pop(), kcat.pop(), (((1,), (1,)), ((), ())),
                                     preferred_element_type=F32)
                att = att + (pm if 2 * m == C else jnp.where(xor < 2 * m, pm, 0.0))
            qs = (qh * jnp.exp2(ch)).astype(BF16)
            o = jnp.dot(att.astype(BF16), vh, preferred_element_type=F32)
            o = o + lax.dot_general(qs, st.astype(BF16), (((1,), (1,)), ((), ())),
                                    preferred_element_type=F32)
            outs.append(o)
        kd = (kh * jnp.exp2(tot - ch)).astype(BF16)
        ut = lax.dot_general(vh, kd, (((0,), (0,)), ((), ())), preferred_element_type=F32)
        st_ref[h] = st * jnp.exp2(tot) + ut
    return outs if want_out else None


def _gla_fwd_kernel(q_ref, k_ref, v_ref, lr_ref, km_ref, vm_ref, lrm_ref, w_ref, b_ref, mats_ref, masks_ref,
                    o_ref, st_ref, c_ref, *, heads, n_meta, cps):
    n = pl.program_id(0)
    common = dict(heads=heads, direction="fwd")

    @pl.when(n == 0)
    def _():
        st_ref[...] = jnp.zeros_like(st_ref)
        g = _log_decay(lrm_ref[...], w_ref, b_ref)
        rows = lax.broadcasted_iota(jnp.int32, g.shape, 0)
        g = jnp.where(rows < n_meta, g, 0.0)
        k = km_ref[...].astype(F32)
        _gla_chunk(None, k, vm_ref[...], g, st_ref, c_ref.at[0], mats_ref, masks_ref, want_out=False, **common)

    @pl.when(n > 0)
    def _():
        for u in range(cps):
            rows = slice(u * GLA_CHUNK, (u + 1) * GLA_CHUNK)
            g = _log_decay(lr_ref[rows, :], w_ref, b_ref)
            q = q_ref[rows, :].astype(F32) * (GLA_DK ** -0.5)
            k = k_ref[rows, :].astype(F32)
            outs = _gla_chunk(q, k, v_ref[rows, :], g, st_ref, c_ref.at[u], mats_ref, masks_ref,
                              want_out=True, **common)
            for h, o in enumerate(outs):
                o_ref[rows, h * GLA_DV:(h + 1) * GLA_DV] = o


def _gla_bwd_kernel(q_ref, k_ref, v_ref, gog_ref, lr_ref, of_ref, w_ref, b_ref, gn_ref, mats_ref, masks_ref,
                    o_ref, st_ref, c_ref, *, heads, cps):
    @pl.when(pl.program_id(0) == 0)
    def _():
        st_ref[...] = jnp.zeros_like(st_ref)

    gn = gn_ref[...]
    for u in reversed(range(cps)):
        rows = slice(u * GLA_CHUNK, (u + 1) * GLA_CHUNK)
        g = _log_decay(lr_ref[rows, :], w_ref, b_ref)
        q = q_ref[rows, :].astype(F32) * (GLA_DK ** -0.5)
        k = k_ref[rows, :].astype(F32)
        outs = _gla_chunk(q, k, v_ref[rows, :], g, st_ref, c_ref.at[u], mats_ref, masks_ref,
                          heads=heads, direction="bwd", want_out=True)
        for h, ob in enumerate(outs):
            vs = slice(h * GLA_DV, (h + 1) * GLA_DV)
            o = _rms(of_ref[rows, vs] + ob, gn)
            o_ref[rows, vs] = (o * jax.nn.silu(gog_ref[rows, vs].astype(F32))).astype(o_ref.dtype)


def _gla(proj_ac, proj_b, km, vm, lrm, wf, bf, wb, bb, gn, *, heads, n_meta, cps=4):
    M = proj_ac.shape[0]
    cps = cps if M % (cps * GLA_CHUNK) == 0 else 1
    C = cps * GLA_CHUNK
    nch = M // C
    qw, vw = heads * GLA_DK, heads * GLA_DV
    const2 = lambda n: (0, 0)
    const3 = lambda n: (0, 0, 0)
    scratch = [pltpu.VMEM((heads, GLA_DV, GLA_DK), F32), pltpu.VMEM((cps, GLA_CHUNK, qw), F32)]
    mats_f, masks_f = _gla_consts("fwd")
    mats_b, masks_b = _gla_consts("bwd")

    xb = lambda n: jnp.maximum(n - 1, 0)
    o_f = pl.pallas_call(
        functools.partial(_gla_fwd_kernel, heads=heads, n_meta=n_meta, cps=cps),
        grid=(nch + 1,),
        in_specs=[
            pl.BlockSpec((C, qw), lambda n: (xb(n), 0)),
            pl.BlockSpec((C, qw), lambda n: (xb(n), 1)),
            pl.BlockSpec((C, vw), lambda n: (xb(n), 1)),
            pl.BlockSpec((C, LANES), lambda n: (xb(n), 0)),
            pl.BlockSpec(km.shape, const2),
            pl.BlockSpec(vm.shape, const2),
            pl.BlockSpec(lrm.shape, const2),
            pl.BlockSpec(wf.shape, const2),
            pl.BlockSpec(bf.shape, const2),
            pl.BlockSpec(mats_f.shape, const2),
            pl.BlockSpec(masks_f.shape, const3),
        ],
        out_specs=pl.BlockSpec((C, vw), lambda n: (xb(n), 0)),
        out_shape=jax.ShapeDtypeStruct((M, vw), F32),
        scratch_shapes=scratch,
        compiler_params=_cparams(("arbitrary",)),
        name="gla_fwd",
    )(proj_ac, proj_ac, proj_ac, proj_b, km, vm, lrm, wf, bf, mats_f, masks_f)

    rb = lambda n: nch - 1 - n
    return pl.pallas_call(
        functools.partial(_gla_bwd_kernel, heads=heads, cps=cps),
        grid=(nch,),
        in_specs=[
            pl.BlockSpec((C, qw), lambda n: (rb(n), 0)),
            pl.BlockSpec((C, qw), lambda n: (rb(n), 1)),
            pl.BlockSpec((C, vw), lambda n: (rb(n), 1)),
            pl.BlockSpec((C, vw), lambda n: (rb(n), 2)),
            pl.BlockSpec((C, LANES), lambda n: (rb(n), 1)),
            pl.BlockSpec((C, vw), lambda n: (rb(n), 0)),
            pl.BlockSpec(wb.shape, const2),
            pl.BlockSpec(bb.shape, const2),
            pl.BlockSpec(gn.shape, const2),
            pl.BlockSpec(mats_b.shape, const2),
            pl.BlockSpec(masks_b.shape, const3),
        ],
        out_specs=pl.BlockSpec((C, vw), lambda n: (rb(n), 0)),
        out_shape=jax.ShapeDtypeStruct((M, vw), BF16),
        scratch_shapes=scratch,
        compiler_params=_cparams(("arbitrary",)),
        name="gla_bwd",
    )(proj_ac, proj_ac, proj_ac, proj_ac, proj_b, o_f, wb, bb, gn, mats_b, masks_b)


def _merge_kernel(og_ref, om_ref, wg_ref, wm_ref, ga_ref, gb_ref, o_ref):
    yg = jnp.dot(og_ref[...], wg_ref[...], preferred_element_type=F32)
    ym = jnp.dot(om_ref[...], wm_ref[...], preferred_element_type=F32)
    ga = jax.nn.sigmoid(ga_ref[...].astype(F32))
    gb = jax.nn.sigmoid(gb_ref[...].astype(F32))
    o_ref[...] = (ga * yg + gb * ym).astype(o_ref.dtype)


def _merge(og, om, wg, wm, gates, *, tm=1024, tn=1024):
    M, D = og.shape[0], wg.shape[1]
    tm, tn = _tile(M, tm), _tile(D, tn)
    gb_blk = D // tn
    return pl.pallas_call(
        _merge_kernel,
        grid=(M // tm, D // tn),
        in_specs=[
            pl.BlockSpec((tm, og.shape[1]), lambda i, j: (i, 0)),
            pl.BlockSpec((tm, om.shape[1]), lambda i, j: (i, 0)),
            pl.BlockSpec((wg.shape[0], tn), lambda i, j: (0, j)),
            pl.BlockSpec((wm.shape[0], tn), lambda i, j: (0, j)),
            pl.BlockSpec((tm, tn), lambda i, j: (i, j)),
            pl.BlockSpec((tm, tn), lambda i, j: (i, j + gb_blk)),
        ],
        out_specs=pl.BlockSpec((tm, tn), lambda i, j: (i, j)),
        out_shape=jax.ShapeDtypeStruct((M, D), BF16),
        compiler_params=_cparams(("arbitrary", "arbitrary")),
        name="merge_out_proj",
    )(og, om, wg, wm, gates, gates)


def _rmm_kernel(a_ref, b_ref, r_ref, o_ref, *acc, nk):
    def part():
        return jnp.dot(a_ref[...], b_ref[...], preferred_element_type=F32)

    if nk == 1:
        o_ref[...] = r_ref[...] + part()
        return
    acc_ref, = acc
    kk = pl.program_id(2)

    @pl.when(kk == 0)
    def _():
        acc_ref[...] = part()

    @pl.when((kk > 0) & (kk < nk - 1))
    def _():
        acc_ref[...] += part()

    @pl.when(kk == nk - 1)
    def _():
        o_ref[...] = r_ref[...] + (acc_ref[...] + part())


def _res_matmul(a, b, res, *, tm=1024, tn=1024, tk=4096, name):
    M, K = a.shape
    N = b.shape[1]
    tm, tn, tk = _tile(M, tm), _tile(N, tn), _tile(K, tk)
    nk = K // tk
    return pl.pallas_call(
        functools.partial(_rmm_kernel, nk=nk),
        grid=(M // tm, N // tn, nk),
        in_specs=[
            pl.BlockSpec((tm, tk), lambda i, j, k: (i, k)),
            pl.BlockSpec((tk, tn), lambda i, j, k: (k, j)),
            pl.BlockSpec((tm, tn), lambda i, j, k: (i, j)),
        ],
        out_specs=pl.BlockSpec((tm, tn), lambda i, j, k: (i, j)),
        out_shape=jax.ShapeDtypeStruct((M, N), F32),
        scratch_shapes=[pltpu.VMEM((tm, tn), F32)] if nk > 1 else [],
        compiler_params=_cparams(("arbitrary", "arbitrary", "arbitrary")),
        name=name,
    )(a, b, res)


def _pad_cols(w, width):
    return jnp.pad(w, ((0, 0), (0, width - w.shape[1])))


def kernel(x, meta_tokens, ln1, w_in, gla_wf, gla_bf, gla_wb, gla_bb, gla_norm, q_norm, w_uq, kv_norm, w_ukv,
           w_gla_out, w_mla_out, w_o, ln2, w_ff1, w_ff2, final_norm):
    assert x.shape[0] == 1 and ln1.shape[0] == 1, "one sequence, one layer"
    S, D = x.shape[1], x.shape[2]
    n_meta = meta_tokens.shape[0]
    rank = gla_wf.shape[1]
    qk_w = gla_wf.shape[2]
    hg = qk_w // GLA_DK
    v_w = hg * GLA_DV
    q_rank, kv_rank = w_uq.shape[1], w_ukv.shape[1]
    hm = w_uq.shape[2] // (MLA_NOPE + MLA_ROPE)
    half = MLA_ROPE // 2
    assert n_meta <= GLA_CHUNK and rank <= LANES and hm % 2 == 0 and S % GLA_CHUNK == 0

    wt = w_in[0].T
    o_gog_end = 2 * qk_w + 2 * v_w
    o_lrb = o_gog_end + rank
    o_cq = o_lrb + rank
    o_ckv = o_cq + q_rank
    o_kr = o_ckv + kv_rank
    o_ga = o_kr + MLA_ROPE
    pad_rows = lambda a: jnp.pad(a, ((0, LANES - a.shape[0]), (0, 0)))
    wt_b = jnp.concatenate([
        pad_rows(wt[o_gog_end:o_lrb]), pad_rows(wt[o_lrb:o_cq]),
        pad_rows(wt[o_kr:o_kr + half]), pad_rows(wt[o_kr + half:o_ga]),
        wt[o_ckv:o_kr], wt[o_cq:o_ckv]], axis=0)
    kr_blk = 1
    ckv_blk = (4 * LANES) // kv_rank
    cq_blk = (4 * LANES + kv_rank) // q_rank
    assert (4 * LANES) % kv_rank == 0 and (4 * LANES + kv_rank) % q_rank == 0

    wq = w_uq[0].reshape(q_rank, hm, MLA_NOPE + MLA_ROPE)
    wq = jnp.concatenate([wq[:, :, :MLA_NOPE].reshape(q_rank, -1), wq[:, :, MLA_NOPE:].reshape(q_rank, -1)],
                         axis=1).astype(BF16)
    wkv = w_ukv[0].astype(BF16)
    wf = jnp.pad(gla_wf[0], ((0, LANES - rank), (0, 0))).astype(BF16)
    wb = jnp.pad(gla_wb[0], ((0, LANES - rank), (0, 0))).astype(BF16)

    inv_freq = ROPE_THETA ** (-jnp.arange(0, MLA_ROPE, 2, dtype=F32) / MLA_ROPE)
    ang = jnp.arange(n_meta + S, dtype=F32)[:, None] * inv_freq[None, :]
    cos, sin = jnp.cos(ang), jnp.sin(ang)
    cos_x, sin_x = cos[n_meta:], sin[n_meta:]
    cos_xp, sin_xp = _pad_cols(cos_x, LANES), _pad_cols(sin_x, LANES)
    mrows = GLA_CHUNK
    padm = lambda a: jnp.pad(a, ((0, mrows - n_meta), (0, LANES - a.shape[1])))
    cos_mp, sin_mp = padm(cos[:n_meta]), padm(sin[:n_meta])

    xs = x[0]
    xm = jnp.pad(meta_tokens.astype(F32), ((0, mrows - n_meta), (0, 0)))
    g1 = ln1[0][None, :]

    xn = _rms_rows(xs, g1, out_dtype=BF16, name="ln1_x")
    xmn = _rms_rows(xm, g1, out_dtype=BF16, name="ln1_meta")
    proj_a = _ws_matmul(xn, wt, col0=0, n_out=o_gog_end, out_dtype=BF16, tm=512, tn=1024, name="in_proj_a")
    gates = _ws_matmul(xn, wt, col0=o_ga, n_out=2 * D, out_dtype=BF16, tm=1024, tn=512, name="in_proj_gates")
    nb = wt_b.shape[0]
    proj_b = _ws_matmul(xn, wt_b, col0=0, n_out=nb, out_dtype=F32, tm=1024, tn=512, name="in_proj_b")
    proj_b_m = _ws_matmul(xmn, wt_b, col0=0, n_out=nb, out_dtype=F32, tm=mrows, tn=512, name="in_proj_b_meta")
    kv_m = _ws_matmul(xmn, wt, col0=qk_w, n_out=qk_w + v_w, out_dtype=BF16, tm=mrows, tn=512,
                      name="in_proj_kv_meta")

    og = _gla(proj_a, proj_b, kv_m[:, :qk_w], kv_m[:, qk_w:], proj_b_m[:, :LANES],
              wf, gla_bf, wb, gla_bb, gla_norm, heads=hg, n_meta=n_meta)

    qt = _q_up(proj_b, cq_blk, q_norm, wq, cos_x.T, sin_x.T, heads=hm)
    gkv = kv_norm
    k_x, vt_x = _kv_up(proj_b, gkv, wkv, cos_xp, sin_xp, heads=hm, ckv_blk=ckv_blk, kr_blk=kr_blk,
                       tm=_tile(S, 512))
    k_m, vt_m = _kv_up(proj_b_m, gkv, wkv, cos_mp, sin_mp, heads=hm, ckv_blk=ckv_blk, kr_blk=kr_blk, tm=mrows)
    om, (wgo, wmo, wo, w1, w2) = _flash(qt, k_x, vt_x, k_m, vt_m,
                                        (w_gla_out[0], w_mla_out[0], w_o[0], w_ff1[0], w_ff2[0]), n_meta=n_meta)

    merged = _merge(og, om, wgo, wmo, gates)
    h1 = _res_matmul(merged, wo, xs, name="o_proj")
    hf = _norm_matmul(h1, ln2[0][None, :], w1, out_dtype=BF16, act="relu2", name="ffn_up")
    h2 = _res_matmul(hf, w2, h1, tk=2048, name="ffn_down")
    return _rms_rows(h2, final_norm[None, :], out_dtype=F32, name="final_norm")[None]
```
